```python
import math
import jax, jax.numpy as jnp
from jax import lax
import numpy as np

D_MODEL = 2048
BATCH = 4
SEQ = 2048
DEPTH = 4

CHUNK = 64
N_MIXERS = 3
N_HEADS = 16
HEAD_DIM = D_MODEL // N_HEADS
Q_BLOCK = 128
LEFT_CHUNKS = 8
BAND = LEFT_CHUNKS + 1
REL_CLIP = 128
CONV_K = 31
D_FF = ((8 * D_MODEL // 3 + 255) // 256) * 256
ALPHA = (2.0 * DEPTH) ** 0.25
BETA = (8.0 * DEPTH) ** -0.25
LN_EPS = 1e-5
N_A = (DEPTH + 2) // 3
N_B = (DEPTH + 1) // 3
N_C = DEPTH // 3

kernel_name = "hybrid_fox_chunkrel_conformer_deepnorm"


def layer_norm(x, g, b):
    xf = x.astype(jnp.float32)
    mu = jnp.mean(xf, axis=-1, keepdims=True)
    var = jnp.mean(jnp.square(xf - mu), axis=-1, keepdims=True)
    y = (xf - mu) * lax.rsqrt(var + LN_EPS) * g.astype(jnp.float32) + b.astype(jnp.float32)
    return y.astype(x.dtype)


def split_heads(qkv):
    B, S, _ = qkv.shape
    q, k, v = jnp.split(qkv, 3, axis=-1)
    shp = (B, S, N_HEADS, HEAD_DIM)
    return q.reshape(shp), k.reshape(shp), v.reshape(shp)


def fox_attention(h, w_qkv, w_f, b_f, w_o):
    B, S, _ = h.shape
    q, k, v = split_heads(h @ w_qkv)
    log_f = jax.nn.log_sigmoid((h @ w_f + b_f).astype(jnp.float32))
    c = jnp.transpose(jnp.cumsum(log_f, axis=1), (0, 2, 1))
    scale = HEAD_DIM ** -0.5
    outs = []
    for blk in range(S // Q_BLOCK):
        q0, q1 = blk * Q_BLOCK, (blk + 1) * Q_BLOCK
        s = jnp.einsum('bqhd,bkhd->bhqk', q[:, q0:q1], k[:, :q1]).astype(jnp.float32) * scale
        s = s + (c[:, :, q0:q1, None] - c[:, :, None, :q1])
        causal = (q0 + jnp.arange(Q_BLOCK))[:, None] >= jnp.arange(q1)[None, :]
        p = jax.nn.softmax(jnp.where(causal[None, None], s, -jnp.inf), axis=-1)
        outs.append(jnp.einsum('bhqk,bkhd->bqhd', p.astype(v.dtype), v[:, :q1]))
    o = jnp.concatenate(outs, axis=1).reshape(B, S, D_MODEL)
    return o @ w_o


def chunk_relpos_attention(h, w_qkv, rel_bias, w_o):
    B, S, _ = h.shape
    nc = S // CHUNK
    q, k, v = split_heads(h @ w_qkv)
    pad = LEFT_CHUNKS * CHUNK
    qc = q.reshape(B, nc, CHUNK, N_HEADS, HEAD_DIM)
    padw = ((0, 0), (pad, 0), (0, 0), (0, 0))
    k_pad = jnp.pad(k, padw).reshape(B, nc + LEFT_CHUNKS, CHUNK, N_HEADS, HEAD_DIM)
    v_pad = jnp.pad(v, padw).reshape(B, nc + LEFT_CHUNKS, CHUNK, N_HEADS, HEAD_DIM)
    band_idx = jnp.arange(nc)[:, None] + jnp.arange(BAND)[None, :]
    kb = k_pad[:, band_idx].reshape(B, nc, BAND * CHUNK, N_HEADS, HEAD_DIM)
    vb = v_pad[:, band_idx].reshape(B, nc, BAND * CHUNK, N_HEADS, HEAD_DIM)
    s = jnp.einsum('bnqhd,bnkhd->bnhqk', qc, kb).astype(jnp.float32) * (HEAD_DIM ** -0.5)
    k_off = jnp.arange(BAND * CHUNK)
    rel = pad + jnp.arange(CHUNK)[:, None] - k_off[None, :]
    bias = rel_bias[:, jnp.clip(rel, -REL_CLIP, REL_CLIP) + REL_CLIP]
    s = s + bias[None, None].astype(jnp.float32)
    valid = (jnp.arange(nc)[:, None] * CHUNK - pad + k_off[None, :]) >= 0
    p = jax.nn.softmax(jnp.where(valid[None, :, None, None, :], s, -jnp.inf), axis=-1)
    o = jnp.einsum('bnhqk,bnkhd->bnqhd', p.astype(vb.dtype), vb).reshape(B, S, D_MODEL)
    return o @ w_o


def conformer_conv(h, w_pw1, b_pw1, w_dw, b_dw, ln_g, ln_b, w_pw2, b_pw2):
    u = h @ w_pw1 + b_pw1
    a, g = jnp.split(u, 2, axis=-1)
    u = a * jax.nn.sigmoid(g)
    y = lax.conv_general_dilated(
        u, w_dw[:, None, :].astype(u.dtype), window_strides=(1,),
        padding=[(CONV_K - 1, 0)], dimension_numbers=('NWC', 'WIO', 'NWC'),
        feature_group_count=D_MODEL) + b_dw
    y = jax.nn.silu(layer_norm(y, ln_g, ln_b))
    return y @ w_pw2 + b_pw2


def swiglu_ffn(h, w_gate, w_up, w_down):
    return (jax.nn.silu(h @ w_gate) * (h @ w_up)) @ w_down


def setup_inputs(seed: int = 0) -> dict:
    key = jax.random.key(seed)
    ks = iter(jax.random.split(key, 32))
    f32 = jnp.float32

    def nrm(shape, scale):
        return jax.random.normal(next(ks), shape, f32) * scale

    D, F, H = D_MODEL, D_FF, N_HEADS
    s_d = D ** -0.5

    def qkv_w(n):
        qk = nrm((n, D, 2 * D), s_d)
        v = nrm((n, D, D), s_d * BETA)
        return jnp.concatenate([qk, v], axis=-1)

    return {
        "x": jax.random.normal(next(ks), (BATCH, SEQ, D), f32),
        "fox_w_qkv": qkv_w(N_A),
        "fox_w_f": nrm((N_A, D, H), s_d),
        "fox_b_f": 3.0 + nrm((N_A, H), 0.5),
        "fox_w_o": nrm((N_A, D, D), s_d * BETA),
        "rel_w_qkv": qkv_w(N_B),
        "rel_bias": nrm((N_B, H, 2 * REL_CLIP + 1), 0.2),
        "rel_w_o": nrm((N_B, D, D), s_d * BETA),
        "conv_w_pw1": nrm((N_C, D, 2 * D), s_d),
        "conv_b_pw1": nrm((N_C, 2 * D), 0.02),
        "conv_w_dw": nrm((N_C, CONV_K, D), CONV_K ** -0.5),
        "conv_b_dw": nrm((N_C, D), 0.02),
        "conv_ln_g": 1.0 + nrm((N_C, D), 0.02),
        "conv_ln_b": nrm((N_C, D), 0.02),
        "conv_w_pw2": nrm((N_C, D, D), s_d * BETA),
        "conv_b_pw2": nrm((N_C, D), 0.02),
        "ffn_w_gate": nrm((DEPTH, D, F), s_d),
        "ffn_w_up": nrm((DEPTH, D, F), s_d),
        "ffn_w_down": nrm((DEPTH, F, D), F ** -0.5 * BETA),
        "ln_mix_g": 1.0 + nrm((DEPTH, D), 0.02),
        "ln_mix_b": nrm((DEPTH, D), 0.02),
        "ln_ffn_g": 1.0 + nrm((DEPTH, D), 0.02),
        "ln_ffn_b": nrm((DEPTH, D), 0.02),
    }


def reference(x, fox_w_qkv, fox_w_f, fox_b_f, fox_w_o,
              rel_w_qkv, rel_bias, rel_w_o,
              conv_w_pw1, conv_b_pw1, conv_w_dw, conv_b_dw, conv_ln_g, conv_ln_b, conv_w_pw2, conv_b_pw2,
              ffn_w_gate, ffn_w_up, ffn_w_down,
              ln_mix_g, ln_mix_b, ln_ffn_g, ln_ffn_b):
    for i in range(DEPTH):
        kind, j = i % N_MIXERS, i // N_MIXERS
        if kind == 0:
            m = fox_attention(x, fox_w_qkv[j], fox_w_f[j], fox_b_f[j], fox_w_o[j])
        elif kind == 1:
            m = chunk_relpos_attention(x, rel_w_qkv[j], rel_bias[j], rel_w_o[j])
        else:
            m = conformer_conv(x, conv_w_pw1[j], conv_b_pw1[j], conv_w_dw[j], conv_b_dw[j],
                               conv_ln_g[j], conv_ln_b[j], conv_w_pw2[j], conv_b_pw2[j])
        x = layer_norm(ALPHA * x + m, ln_mix_g[i], ln_mix_b[i])
        f = swiglu_ffn(x, ffn_w_gate[i], ffn_w_up[i], ffn_w_down[i])
        x = layer_norm(ALPHA * x + f, ln_ffn_g[i], ln_ffn_b[i])
    return x
```

```python
import functools

import numpy as np
import jax
import jax.numpy as jnp
from jax import lax
from jax.experimental import pallas as pl
from jax.experimental.pallas import tpu as pltpu

F32 = jnp.float32
BF16 = jnp.bfloat16

N_HEADS = 16
HEAD_DIM = 128
CHUNK = 64
LEFT_CHUNKS = 8
REL_CLIP = 128
CONV_K = 31
DEPTH = 4
N_MIXERS = 3
LN_EPS = 1e-5
ALPHA = (2.0 * DEPTH) ** 0.25

LANES = 128
V7X_VMEM_LIMIT = 56 * 1024 * 1024

N_SPLIT = 3
GATE_COLS = 2 * N_SPLIT

REL_TQ = 4 * CHUNK
REL_TK = REL_TQ + LEFT_CHUNKS * CHUNK
REL_KBLOCKS = REL_TK // REL_TQ
REL_ROLL_W = 1024
CONV_HALO = 32


def _params(n_axes, vmem_bytes):
    limit = int(min(max(vmem_bytes * 5 // 4 + (4 << 20), 16 << 20), V7X_VMEM_LIMIT))
    return pltpu.CompilerParams(dimension_semantics=("arbitrary",) * n_axes,
                                vmem_limit_bytes=limit)


def _layer_norm(y, g, b):
    mu = jnp.mean(y, axis=-1, keepdims=True)
    d = y - mu
    var = jnp.mean(d * d, axis=-1, keepdims=True)
    return d * lax.rsqrt(var + LN_EPS) * g + b


def _split3(x):
    hi = x.astype(BF16)
    r = x - hi.astype(F32)
    mid = r.astype(BF16)
    lo = (r - mid.astype(F32)).astype(BF16)
    return hi, mid, lo


def _gated_mm_kernel(*refs, glu, has_bias):
    if has_bias:
        x_ref, wa_ref, wb_ref, ba_ref, bb_ref, o_ref = refs
    else:
        x_ref, wa_ref, wb_ref, o_ref = refs
    x = x_ref[...]
    a = jnp.dot(x, wa_ref[...], preferred_element_type=F32)
    b = jnp.dot(x, wb_ref[...], preferred_element_type=F32)
    if has_bias:
        a = a + ba_ref[...]
        b = b + bb_ref[...]
    if glu:
        out = a * jax.nn.sigmoid(b)
    else:
        out = (a * jax.nn.sigmoid(a)) * b
    o_ref[...] = out.astype(o_ref.dtype)


def _gated_mm(x, wa, wb, *, n_out, a_off, b_off, glu, bias=None, out_dtype=BF16, tm=1024, tn=512):
    m, k = x.shape
    tm = min(tm, m)
    assert m % tm == 0 and n_out % tn == 0 and a_off % tn == 0 and b_off % tn == 0
    in_specs = [
        pl.BlockSpec((tm, k), lambda i, j: (i, 0)),
        pl.BlockSpec((k, tn), lambda i, j: (0, j + a_off // tn)),
        pl.BlockSpec((k, tn), lambda i, j: (0, j + b_off // tn)),
    ]
    args = [x, wa, wb]
    if bias is not None:
        in_specs += [pl.BlockSpec((1, tn), lambda i, j: (0, j + a_off // tn)),
                     pl.BlockSpec((1, tn), lambda i, j: (0, j + b_off // tn))]
        args += [bias, bias]
    vmem = 2 * (tm * k * 2 + 2 * k * tn * 2 + tm * tn * jnp.dtype(out_dtype).itemsize) + 3 * tm * tn * 4
    return pl.pallas_call(
        functools.partial(_gated_mm_kernel, glu=glu, has_bias=bias is not None),
        grid=(m // tm, n_out // tn),
        in_specs=in_specs,
        out_specs=pl.BlockSpec((tm, tn), lambda i, j: (i, j)),
        out_shape=jax.ShapeDtypeStruct((m, n_out), out_dtype),
        compiler_params=_params(2, vmem),
        name="gated_mm_glu" if glu else "gated_mm_swiglu",
    )(*args)


def _qkv_kernel(*refs, q_tiles, scale, has_gate):
    if has_gate:
        x_ref, w_ref, wf_ref, o_ref, z_ref = refs
    else:
        x_ref, w_ref, o_ref = refs
    j = pl.program_id(1)
    acc = jnp.dot(x_ref[...], w_ref[...], preferred_element_type=F32)
    sc = jnp.where(j < q_tiles, scale, 1.0).astype(F32)
    o_ref[...] = (acc * sc).astype(o_ref.dtype)
    if has_gate:
        @pl.when(j == 0)
        def _():
            z_ref[...] = jnp.dot(x_ref[...], wf_ref[...], preferred_element_type=F32)


def _qkv_proj(x, w, wf=None, *, tm=1024, tn=512):
    m, k = x.shape
    n = w.shape[1]
    d_model = n // 3
    tm = min(tm, m)
    assert m % tm == 0 and n % tn == 0 and d_model % tn == 0
    has_gate = wf is not None
    in_specs = [pl.BlockSpec((tm, k), lambda i, j: (i, 0)),
                pl.BlockSpec((k, tn), lambda i, j: (0, j))]
    out_specs = [pl.BlockSpec((tm, tn), lambda i, j: (i, j))]
    out_shape = [jax.ShapeDtypeStruct((m, n), BF16)]
    args = [x, w]
    if has_gate:
        in_specs.append(pl.BlockSpec((k, LANES), lambda i, j: (0, 0)))
        out_specs.append(pl.BlockSpec((tm, LANES), lambda i, j: (i, 0)))
        out_shape.append(jax.ShapeDtypeStruct((m, LANES), F32))
        args.append(wf)
    vmem = 2 * (tm * k * 2 + k * tn * 2 + tm * tn * 2 + k * LANES * 2 + tm * LANES * 4) + 2 * tm * tn * 4
    out = pl.pallas_call(
        functools.partial(_qkv_kernel, q_tiles=d_model // tn, scale=HEAD_DIM ** -0.5, has_gate=has_gate),
        grid=(m // tm, n // tn),
        in_specs=in_specs,
        out_specs=out_specs,
        out_shape=out_shape,
        compiler_params=_params(2, vmem),
        name="qkv_proj_gate" if has_gate else "qkv_proj",
    )(*args)
    return out if has_gate else (out[0], None)


def _mm_res_ln_kernel(*refs, nk, has_bias, want_bf16):
    refs = list(refs)
    h_ref, w_ref = refs[:2]
    pos = 2
    bias_ref = None
    if has_bias:
        bias_ref = refs[pos]
        pos += 1
    x_ref, g_ref, b_ref = refs[pos:pos + 3]
    pos += 3
    of_ref = refs[pos]
    pos += 1
    ob_ref = None
    if want_bf16:
        ob_ref = refs[pos]
        pos += 1
    acc_ref = refs[pos] if nk > 1 else None

    def epilogue(acc):
        y = ALPHA * x_ref[...] + acc
        if has_bias:
            y = y + bias_ref[...]
        out = _layer_norm(y, g_ref[...], b_ref[...])
        of_ref[...] = out
        if want_bf16:
            ob_ref[...] = out.astype(BF16)

    part = jnp.dot(h_ref[...], w_ref[...], preferred_element_type=F32)
    if nk == 1:
        epilogue(part)
        return
    kk = pl.program_id(1)

    @pl.when(kk == 0)
    def _():
        acc_ref[...] = jnp.zeros_like(acc_ref)

    acc_ref[...] += part

    @pl.when(kk == nk - 1)
    def _():
        epilogue(acc_ref[...])


def _mm_res_ln(h, w, x_res, ln_g, ln_b, *, bias=None, want_bf16=True, tm=512, tk=512):
    m, k = h.shape
    n = w.shape[1]
    assert m % tm == 0 and k % tk == 0
    nk = k // tk
    in_specs = [pl.BlockSpec((tm, tk), lambda i, kk: (i, kk)),
                pl.BlockSpec((tk, n), lambda i, kk: (kk, 0))]
    args = [h, w]
    row = pl.BlockSpec((1, n), lambda i, kk: (0, 0))
    if bias is not None:
        in_specs.append(row)
        args.append(bias)
    in_specs += [pl.BlockSpec((tm, n), lambda i, kk: (i, 0)), row, row]
    args += [x_res, ln_g, ln_b]
    out_specs = [pl.BlockSpec((tm, n), lambda i, kk: (i, 0))]
    out_shape = [jax.ShapeDtypeStruct((m, n), F32)]
    if want_bf16:
        out_specs.append(pl.BlockSpec((tm, n), lambda i, kk: (i, 0)))
        out_shape.append(jax.ShapeDtypeStruct((m, n), BF16))
    scratch = [pltpu.VMEM((tm, n), F32)] if nk > 1 else []
    vmem = 2 * (tm * tk * 2 + tk * n * 2 + tm * n * 4 + tm * n * 4 + tm * n * 2) + 3 * tm * n * 4
    out = pl.pallas_call(
        functools.partial(_mm_res_ln_kernel, nk=nk, has_bias=bias is not None, want_bf16=want_bf16),
        grid=(m // tm, nk),
        in_specs=in_specs,
        out_specs=out_specs,
        out_shape=out_shape,
        scratch_shapes=scratch,
        compiler_params=_params(2, vmem),
        name="mm_res_ln",
    )(*args)
    return (out[0], out[1]) if want_bf16 else (out[0], None)


def _gate_placement():
    pq = np.zeros((N_SPLIT, LANES, LANES), np.float32)
    pk = np.zeros((N_SPLIT, LANES, LANES), np.float32)
    cq = np.zeros((1, LANES), np.float32)
    ck = np.zeros((1, LANES), np.float32)
    dsel = np.zeros((N_HEADS, LANES, LANES), np.float32)
    for h in range(N_HEADS):
        base = GATE_COLS * h
        for p in range(N_SPLIT):
            pq[p, h, base + p] = 1.0
            ck[0, base + p] = 1.0
            pk[p, h, base + N_SPLIT + p] = 1.0
            cq[0, base + N_SPLIT + p] = 1.0
        for cidx in range(GATE_COLS):
            dsel[h, base + cidx, base + cidx] = 1.0
    return (jnp.asarray(pq, BF16), jnp.asarray(pk, BF16), jnp.asarray(cq, F32),
            jnp.asarray(ck, F32), jnp.asarray(dsel, BF16))


def _fox_gate_kernel(z_ref, bf_ref, pq_ref, pk_ref, cq_ref, ck_ref, qx_ref, kx_ref):
    x = z_ref[...] + bf_ref[...]
    lf = -(jnp.maximum(-x, 0.0) + jnp.log1p(jnp.exp(-jnp.abs(x))))
    n = lf.shape[0]
    row = lax.broadcasted_iota(jnp.int32, lf.shape, 0)
    c = lf
    sh = 1
    while sh < n:
        c = c + jnp.where(row >= sh, pltpu.roll(c, sh, 0), 0.0)
        sh *= 2
    parts = _split3(c)
    qx = cq_ref[...]
    kx = ck_ref[...]
    for p in range(N_SPLIT):
        qx = qx + jnp.dot(parts[p], pq_ref[p], preferred_element_type=F32)
        kx = kx - jnp.dot(parts[p], pk_ref[p], preferred_element_type=F32)
    qx_ref[...] = qx.astype(BF16)
    kx_ref[...] = kx.astype(BF16)


def _fox_gate(z, bf_row, consts, *, seq):
    m = z.shape[0]
    pq, pk, cq, ck, _ = consts
    rows = pl.BlockSpec((seq, LANES), lambda b: (b, 0))
    vec = pl.BlockSpec((1, LANES), lambda b: (0, 0))
    mats = pl.BlockSpec((N_SPLIT, LANES, LANES), lambda b: (0, 0, 0))
    vmem = 16 * seq * LANES * 4
    return pl.pallas_call(
        _fox_gate_kernel,
        grid=(m // seq,),
        in_specs=[rows, vec, mats, mats, vec, vec],
        out_specs=[rows, rows],
        out_shape=[jax.ShapeDtypeStruct((m, LANES), BF16)] * 2,
        compiler_params=_params(1, vmem),
        name="fox_gate",
    )(z, bf_row, pq, pk, cq, ck)


def _fox_attn_kernel(q_ref, k_ref, v_ref, qx_ref, kx_ref, dsel_ref, o_ref, kaug_ref, *, tq):
    qi = pl.program_id(2)

    @pl.when(qi == 0)
    def _():
        kaug_ref[:, :HEAD_DIM] = k_ref[...]
        kaug_ref[:, HEAD_DIM:] = jnp.dot(kx_ref[...], dsel_ref[...],
                                         preferred_element_type=F32).astype(BF16)

    q_aug = jnp.concatenate([q_ref[...], qx_ref[...]], axis=1)

    def scores(start):
        k_blk = kaug_ref[pl.ds(start, tq), :]
        return lax.dot_general(q_aug, k_blk, (((1,), (1,)), ((), ())), preferred_element_type=F32)

    def update(carry, s, start):
        m, l, acc = carry
        m_new = jnp.maximum(m, jnp.max(s, axis=-1, keepdims=True))
        p = jnp.exp(s - m_new)
        a = jnp.exp(m - m_new)
        l = a * l + jnp.sum(p, axis=-1, keepdims=True)
        v_blk = v_ref[pl.ds(start, tq), :]
        acc = a * acc + jnp.dot(p.astype(BF16), v_blk, preferred_element_type=F32)
        return m_new, l, acc

    def body(kj, carry):
        start = pl.multiple_of(kj * tq, tq)
        return update(carry, scores(start), start)

    init = (jnp.full((tq, 1), -jnp.inf, F32), jnp.zeros((tq, 1), F32), jnp.zeros((tq, HEAD_DIM), F32))
    carry = lax.fori_loop(0, qi, body, init)
    start = pl.multiple_of(qi * tq, tq)
    s = scores(start)
    r = lax.broadcasted_iota(jnp.int32, (tq, tq), 0)
    c = lax.broadcasted_iota(jnp.int32, (tq, tq), 1)
    s = jnp.where(r >= c, s, -jnp.inf)
    _, l, acc = update(carry, s, start)
    o_ref[...] = (acc / l).astype(o_ref.dtype)


def _fox_attn(qkv, qx, kx, dsel, *, batch, seq, tq=256):
    d_model = N_HEADS * HEAD_DIM
    nq = seq // tq
    qblk = lambda off: pl.BlockSpec((tq, HEAD_DIM), lambda b, h, qi: (b * nq + qi, h + off))
    full = lambda off: pl.BlockSpec((seq, HEAD_DIM), lambda b, h, qi: (b, h + off))
    vmem = 2 * (4 * seq * HEAD_DIM * 2 + 3 * tq * HEAD_DIM * 2) + seq * 2 * HEAD_DIM * 2 + 8 * tq * tq * 4
    return pl.pallas_call(
        functools.partial(_fox_attn_kernel, tq=tq),
        grid=(batch, N_HEADS, nq),
        in_specs=[qblk(0), full(N_HEADS), full(2 * N_HEADS),
                  pl.BlockSpec((tq, LANES), lambda b, h, qi: (b * nq + qi, 0)),
                  pl.BlockSpec((seq, LANES), lambda b, h, qi: (b, 0)),
                  pl.BlockSpec((None, LANES, LANES), lambda b, h, qi: (h, 0, 0))],
        out_specs=pl.BlockSpec((tq, HEAD_DIM), lambda b, h, qi: (b * nq + qi, h)),
        out_shape=jax.ShapeDtypeStruct((batch * seq, d_model), BF16),
        scratch_shapes=[pltpu.VMEM((seq, 2 * HEAD_DIM), BF16)],
        compiler_params=_params(3, vmem),
        name="fox_attn",
    )(qkv, qkv, qkv, qx, kx, dsel)


def _rel_bias_kernel(rb_ref, o_ref):
    rb = jnp.broadcast_to(rb_ref[...], (8, rb_ref.shape[-1]))
    nrb = rb.shape[1]
    ridx = lax.broadcasted_iota(jnp.int32, (nrb, REL_ROLL_W), 0)
    mcol = lax.broadcasted_iota(jnp.int32, (nrb, REL_ROLL_W), 1)
    off = jnp.where(mcol < REL_TK, mcol, mcol - REL_ROLL_W)
    idx = jnp.clip(LEFT_CHUNKS * CHUNK - off, -REL_CLIP, REL_CLIP) + REL_CLIP
    onehot = jnp.where(ridx == idx, 1.0, 0.0).astype(BF16)
    vec = jnp.zeros((8, REL_ROLL_W), F32)
    for part in _split3(rb):
        vec = vec + jnp.dot(part, onehot, preferred_element_type=F32)
    tile = jnp.broadcast_to(vec[0:1, :], (REL_TQ, REL_ROLL_W))
    tile = pltpu.roll(tile, 0, 1, stride=1, stride_axis=0)[:, :REL_TK]
    qc = jnp.right_shift(lax.broadcasted_iota(jnp.int32, (REL_TQ, REL_TK), 0), 6)
    kc = jnp.right_shift(lax.broadcasted_iota(jnp.int32, (REL_TQ, REL_TK), 1), 6)
    valid = (kc >= qc) & (kc <= qc + LEFT_CHUNKS)
    o_ref[...] = jnp.where(valid, tile, -jnp.inf)


def _rel_bias_tiles(rel_bias):
    h, nb = rel_bias.shape
    nbp = -(-nb // LANES) * LANES
    rb = jnp.pad(rel_bias, ((0, 0), (0, nbp - nb))).reshape(h, 1, nbp)
    vmem = 8 * nbp * REL_ROLL_W * 4 + 8 * REL_TQ * REL_ROLL_W * 4
    return pl.pallas_call(
        _rel_bias_kernel,
        grid=(h,),
        in_specs=[pl.BlockSpec((None, 1, nbp), lambda i: (i, 0, 0))],
        out_specs=pl.BlockSpec((None, REL_TQ, REL_TK), lambda i: (i, 0, 0)),
        out_shape=jax.ShapeDtypeStruct((h, REL_TQ, REL_TK), F32),
        compiler_params=_params(1, vmem),
        name="rel_bias_tiles",
    )(rb)


def _rel_attn_kernel(q_ref, k0_ref, k1_ref, k2_ref, v0_ref, v1_ref, v2_ref, bias_ref, o_ref):
    qb = pl.program_id(2)
    k = jnp.concatenate([k0_ref[...], k1_ref[...], k2_ref[...]], axis=0)
    v = jnp.concatenate([v0_ref[...], v1_ref[...], v2_ref[...]], axis=0)
    s = lax.dot_general(q_ref[...], k, (((1,), (1,)), ((), ())), preferred_element_type=F32)
    s = s + bias_ref[...]
    col = lax.broadcasted_iota(jnp.int32, s.shape, 1)
    s = jnp.where(col >= (REL_KBLOCKS - 1 - qb) * REL_TQ, s, -jnp.inf)
    m = jnp.max(s, axis=-1, keepdims=True)
    p = jnp.exp(s - m)
    l = jnp.sum(p, axis=-1, keepdims=True)
    o = jnp.dot(p.astype(BF16), v, preferred_element_type=F32)
    o_ref[...] = (o / l).astype(o_ref.dtype)


def _rel_attn(qkv, bias_tiles, *, batch, seq):
    d_model = N_HEADS * HEAD_DIM
    nqb = seq // REL_TQ
    blk = (REL_TQ, HEAD_DIM)

    def kv_spec(d, off):
        return pl.BlockSpec(
            blk, lambda h, b, qb: (b * nqb + jnp.maximum(qb - (REL_KBLOCKS - 1) + d, 0), h + off))

    in_specs = [pl.BlockSpec(blk, lambda h, b, qb: (b * nqb + qb, h))]
    in_specs += [kv_spec(d, N_HEADS) for d in range(REL_KBLOCKS)]
    in_specs += [kv_spec(d, 2 * N_HEADS) for d in range(REL_KBLOCKS)]
    in_specs.append(pl.BlockSpec((None, REL_TQ, REL_TK), lambda h, b, qb: (h, 0, 0)))
    vmem = 2 * (8 * REL_TQ * HEAD_DIM * 2 + REL_TQ * REL_TK * 4) + 6 * REL_TQ * REL_TK * 4
    return pl.pallas_call(
        _rel_attn_kernel,
        grid=(N_HEADS, batch, nqb),
        in_specs=in_specs,
        out_specs=pl.BlockSpec(blk, lambda h, b, qb: (b * nqb + qb, h)),
        out_shape=jax.ShapeDtypeStruct((batch * seq, d_model), BF16),
        compiler_params=_params(3, vmem),
        name="rel_attn",
    )(*([qkv] * (1 + 2 * REL_KBLOCKS)), bias_tiles)


def _conv_tail_kernel(u_ref, halo_ref, wdw_ref, bdw_ref, cg_ref, cb_ref, w2_ref, b2_ref,
                      x_ref, g_ref, b_ref, of_ref, ob_ref, ext_ref, y_ref, *, tm, tiles_per_seq):
    i = pl.program_id(0)
    d_model = u_ref.shape[1]
    keep = jnp.where(i % tiles_per_seq == 0, 0.0, 1.0).astype(F32)
    ext_ref[0:CONV_HALO, :] = halo_ref[...] * keep
    ext_ref[CONV_HALO:CONV_HALO + tm, :] = u_ref[...]
    base = CONV_HALO - (CONV_K - 1)
    for c0 in range(0, d_model, LANES):
        cols = slice(c0, c0 + LANES)
        acc = jnp.broadcast_to(bdw_ref[:, cols], (tm, LANES))
        for k in range(CONV_K):
            acc = acc + wdw_ref[k:k + 1, cols] * ext_ref[base + k:base + k + tm, cols]
        y_ref[:, cols] = acc
    y = _layer_norm(y_ref[...], cg_ref[...], cb_ref[...])
    y = y * jax.nn.sigmoid(y)
    mix = jnp.dot(y.astype(BF16), w2_ref[...], preferred_element_type=F32) + b2_ref[...]
    out = _layer_norm(ALPHA * x_ref[...] + mix, g_ref[...], b_ref[...])
    of_ref[...] = out
    ob_ref[...] = out.astype(BF16)


def _conv_tail(u, w_dw, b_dw, cg, cb, w2, b2, x_res, ln_g, ln_b, *, seq, tm=256):
    m, d = u.shape
    assert seq % tm == 0 and tm % CONV_HALO == 0
    hb = tm // CONV_HALO
    row = pl.BlockSpec((1, d), lambda i: (0, 0))
    tile = pl.BlockSpec((tm, d), lambda i: (i, 0))
    wdw = jnp.pad(w_dw, ((0, CONV_HALO - CONV_K), (0, 0)))
    vmem = 2 * (tm * d * 4 * 3 + tm * d * 2 + CONV_HALO * d * 8 + d * d * 2) + (2 * tm + CONV_HALO) * d * 4 \
        + 4 * tm * d * 4
    return pl.pallas_call(
        functools.partial(_conv_tail_kernel, tm=tm, tiles_per_seq=seq // tm),
        grid=(m // tm,),
        in_specs=[tile,
                  pl.BlockSpec((CONV_HALO, d), lambda i: (jnp.maximum(i * hb - 1, 0), 0)),
                  pl.BlockSpec((CONV_HALO, d), lambda i: (0, 0)),
                  row, row, row,
                  pl.BlockSpec((d, d), lambda i: (0, 0)),
                  row, tile, row, row],
        out_specs=[tile, tile],
        out_shape=[jax.ShapeDtypeStruct((m, d), F32), jax.ShapeDtypeStruct((m, d), BF16)],
        scratch_shapes=[pltpu.VMEM((CONV_HALO + tm, d), F32), pltpu.VMEM((tm, d), F32)],
        compiler_params=_params(1, vmem),
        name="conv_tail",
    )(u, u, wdw, b_dw, cg, cb, w2, b2, x_res, ln_g, ln_b)


def kernel(x, fox_w_qkv, fox_w_f, fox_b_f, fox_w_o, rel_w_qkv, rel_bias, rel_w_o, conv_w_pw1, conv_b_pw1, conv_w_dw, conv_b_dw, conv_ln_g, conv_ln_b, conv_w_pw2, conv_b_pw2, ffn_w_gate, ffn_w_up, ffn_w_down, ln_mix_g, ln_mix_b, ln_ffn_g, ln_ffn_b):
    batch, seq, d = x.shape
    m = batch * seq
    assert d == N_HEADS * HEAD_DIM
    row = lambda v: v.reshape(1, -1).astype(F32)
    xf = x.reshape(m, d)
    xb = xf.astype(BF16)
    gate_consts = _gate_placement()

    for i in range(DEPTH):
        kind, j = i % N_MIXERS, i // N_MIXERS
        mix_g, mix_b = row(ln_mix_g[i]), row(ln_mix_b[i])
        if kind == 0:
            wf = jnp.pad(fox_w_f[j], ((0, 0), (0, LANES - N_HEADS))).astype(BF16)
            bf = jnp.pad(fox_b_f[j], (0, LANES - N_HEADS)).reshape(1, LANES).astype(F32)
            qkv, z = _qkv_proj(xb, fox_w_qkv[j].astype(BF16), wf)
            qx, kx = _fox_gate(z, bf, gate_consts, seq=seq)
            o = _fox_attn(qkv, qx, kx, gate_consts[4], batch=batch, seq=seq)
            xf, xb = _mm_res_ln(o, fox_w_o[j].astype(BF16), xf, mix_g, mix_b, tk=d)
        elif kind == 1:
            qkv, _ = _qkv_proj(xb, rel_w_qkv[j].astype(BF16))
            tiles = _rel_bias_tiles(rel_bias[j])
            o = _rel_attn(qkv, tiles, batch=batch, seq=seq)
            xf, xb = _mm_res_ln(o, rel_w_o[j].astype(BF16), xf, mix_g, mix_b, tk=d)
        else:
            w1 = conv_w_pw1[j].astype(BF16)
            u = _gated_mm(xb, w1, w1, n_out=d, a_off=0, b_off=d, glu=True,
                          bias=row(conv_b_pw1[j]), out_dtype=F32)
            xf, xb = _conv_tail(u, conv_w_dw[j], row(conv_b_dw[j]), row(conv_ln_g[j]), row(conv_ln_b[j]),
                                conv_w_pw2[j].astype(BF16), row(conv_b_pw2[j]), xf, mix_g, mix_b, seq=seq)
        hdn = _gated_mm(xb, ffn_w_gate[i].astype(BF16), ffn_w_up[i].astype(BF16),
                        n_out=ffn_w_gate.shape[2], a_off=0, b_off=0, glu=False)
        xf, xb = _mm_res_ln(hdn, ffn_w_down[i].astype(BF16), xf, row(ln_ffn_g[i]), row(ln_ffn_b[i]),
                            want_bf16=i + 1 < DEPTH)
    return xf.reshape(batch, seq, d)
```

```python
import functools

import numpy as np
import jax
import jax.numpy as jnp
from jax import lax
from jax.experimental import pallas as pl
from jax.experimental.pallas import tpu as pltpu

F32 = jnp.float32
BF16 = jnp.bfloat16

N_HEADS = 16
HEAD_DIM = 128
CHUNK = 64
LEFT_CHUNKS = 8
REL_CLIP = 128
CONV_K = 31
DEPTH = 4
N_MIXERS = 3
LN_EPS = 1e-5
ALPHA = (2.0 * DEPTH) ** 0.25

LANES = 128
V7X_VMEM_LIMIT = 56 * 1024 * 1024

N_SPLIT = 3
GATE_COLS = 2 * N_SPLIT

REL_TQ = 4 * CHUNK
REL_TK = REL_TQ + LEFT_CHUNKS * CHUNK
REL_KBLOCKS = REL_TK // REL_TQ
REL_ROLL_W = 1024
CONV_HALO = 32


def _params(n_axes, vmem_bytes):
    limit = int(min(max(vmem_bytes * 5 // 4 + (4 << 20), 16 << 20), V7X_VMEM_LIMIT))
    return pltpu.CompilerParams(dimension_semantics=("arbitrary",) * n_axes,
                                vmem_limit_bytes=limit)


def _layer_norm(y, g, b):
    mu = jnp.mean(y, axis=-1, keepdims=True)
    d = y - mu
    var = jnp.mean(d * d, axis=-1, keepdims=True)
    return d * lax.rsqrt(var + LN_EPS) * g + b


def _split3(x):
    hi = x.astype(BF16)
    r = x - hi.astype(F32)
    mid = r.astype(BF16)
    lo = (r - mid.astype(F32)).astype(BF16)
    return hi, mid, lo


def _gated_mm_kernel(*refs, glu, has_bias):
    if has_bias:
        x_ref, wa_ref, wb_ref, ba_ref, bb_ref, o_ref, wa_bf_ref, wb_bf_ref = refs
    else:
        x_ref, wa_ref, wb_ref, o_ref, wa_bf_ref, wb_bf_ref = refs

    @pl.when(pl.program_id(1) == 0)
    def _():
        wa_bf_ref[...] = wa_ref[...].astype(BF16)
        wb_bf_ref[...] = wb_ref[...].astype(BF16)

    x = x_ref[...]
    a = jnp.dot(x, wa_bf_ref[...], preferred_element_type=F32)
    b = jnp.dot(x, wb_bf_ref[...], preferred_element_type=F32)
    if has_bias:
        a = a + ba_ref[...]
        b = b + bb_ref[...]
    if glu:
        out = a * jax.nn.sigmoid(b)
    else:
        out = (a * jax.nn.sigmoid(a)) * b
    o_ref[...] = out.astype(o_ref.dtype)


def _gated_mm(x, wa, wb, layer, *, n_out, a_off, b_off, glu, bias=None, out_dtype=BF16, tm=1024, tn=512):
    m, k = x.shape
    tm = min(tm, m)
    assert m % tm == 0 and n_out % tn == 0 and a_off % tn == 0 and b_off % tn == 0
    in_specs = [
        pl.BlockSpec((tm, k), lambda j, i: (i, 0)),
        pl.BlockSpec((None, k, tn), lambda j, i: (layer, 0, j + a_off // tn)),
        pl.BlockSpec((None, k, tn), lambda j, i: (layer, 0, j + b_off // tn)),
    ]
    args = [x, wa, wb]
    if bias is not None:
        in_specs += [pl.BlockSpec((1, tn), lambda j, i: (0, j + a_off // tn)),
                     pl.BlockSpec((1, tn), lambda j, i: (0, j + b_off // tn))]
        args += [bias, bias]
    vmem = 2 * (tm * k * 2 + 2 * k * tn * 4 + tm * tn * jnp.dtype(out_dtype).itemsize) \
        + 2 * k * tn * 2 + 3 * tm * tn * 4
    return pl.pallas_call(
        functools.partial(_gated_mm_kernel, glu=glu, has_bias=bias is not None),
        grid=(n_out // tn, m // tm),
        in_specs=in_specs,
        out_specs=pl.BlockSpec((tm, tn), lambda j, i: (i, j)),
        out_shape=jax.ShapeDtypeStruct((m, n_out), out_dtype),
        scratch_shapes=[pltpu.VMEM((k, tn), BF16), pltpu.VMEM((k, tn), BF16)],
        compiler_params=_params(2, vmem),
        name="gated_mm_glu" if glu else "gated_mm_swiglu",
    )(*args)


def _qkv_kernel(x_ref, w_ref, o_ref, w_bf_ref, *, q_tiles, scale):
    @pl.when(pl.program_id(1) == 0)
    def _():
        w_bf_ref[...] = w_ref[...].astype(BF16)

    acc = jnp.dot(x_ref[...], w_bf_ref[...], preferred_element_type=F32)
    sc = jnp.where(pl.program_id(0) < q_tiles, scale, 1.0).astype(F32)
    o_ref[...] = (acc * sc).astype(o_ref.dtype)


def _qkv_proj(x, w, layer, *, tm=1024, tn=512):
    m, k = x.shape
    n = w.shape[2]
    d_model = n // 3
    tm = min(tm, m)
    assert m % tm == 0 and n % tn == 0 and d_model % tn == 0
    vmem = 2 * (tm * k * 2 + k * tn * 4 + tm * tn * 2) + k * tn * 2 + 2 * tm * tn * 4
    return pl.pallas_call(
        functools.partial(_qkv_kernel, q_tiles=d_model // tn, scale=HEAD_DIM ** -0.5),
        grid=(n // tn, m // tm),
        in_specs=[pl.BlockSpec((tm, k), lambda j, i: (i, 0)),
                  pl.BlockSpec((None, k, tn), lambda j, i: (layer, 0, j))],
        out_specs=pl.BlockSpec((tm, tn), lambda j, i: (i, j)),
        out_shape=jax.ShapeDtypeStruct((m, n), BF16),
        scratch_shapes=[pltpu.VMEM((k, tn), BF16)],
        compiler_params=_params(2, vmem),
        name="qkv_proj",
    )(x, w)


def _mm_res_ln_kernel(*refs, nk, has_bias, want_bf16):
    refs = list(refs)
    h_ref, w_ref = refs[:2]
    pos = 2
    bias_ref = None
    if has_bias:
        bias_ref = refs[pos]
        pos += 1
    x_ref, g_ref, b_ref = refs[pos:pos + 3]
    pos += 3
    of_ref = refs[pos]
    pos += 1
    ob_ref = None
    if want_bf16:
        ob_ref = refs[pos]
        pos += 1
    acc_ref = refs[pos] if nk > 1 else None

    def epilogue(acc):
        y = ALPHA * x_ref[...] + acc
        if has_bias:
            y = y + bias_ref[...]
        out = _layer_norm(y, g_ref[...], b_ref[...])
        of_ref[...] = out
        if want_bf16:
            ob_ref[...] = out.astype(BF16)

    part = jnp.dot(h_ref[...], w_ref[...], preferred_element_type=F32)
    if nk == 1:
        epilogue(part)
        return
    kk = pl.program_id(1)

    @pl.when(kk == 0)
    def _():
        acc_ref[...] = jnp.zeros_like(acc_ref)

    acc_ref[...] += part

    @pl.when(kk == nk - 1)
    def _():
        epilogue(acc_ref[...])


def _mm_res_ln(h, w, x_res, ln_g, ln_b, *, bias=None, want_bf16=True, tm=512, tk=512):
    m, k = h.shape
    n = w.shape[1]
    assert m % tm == 0 and k % tk == 0
    nk = k // tk
    in_specs = [pl.BlockSpec((tm, tk), lambda i, kk: (i, kk)),
                pl.BlockSpec((tk, n), lambda i, kk: (kk, 0))]
    args = [h, w]
    row = pl.BlockSpec((1, n), lambda i, kk: (0, 0))
    if bias is not None:
        in_specs.append(row)
        args.append(bias)
    in_specs += [pl.BlockSpec((tm, n), lambda i, kk: (i, 0)), row, row]
    args += [x_res, ln_g, ln_b]
    out_specs = [pl.BlockSpec((tm, n), lambda i, kk: (i, 0))]
    out_shape = [jax.ShapeDtypeStruct((m, n), F32)]
    if want_bf16:
        out_specs.append(pl.BlockSpec((tm, n), lambda i, kk: (i, 0)))
        out_shape.append(jax.ShapeDtypeStruct((m, n), BF16))
    scratch = [pltpu.VMEM((tm, n), F32)] if nk > 1 else []
    vmem = 2 * (tm * tk * 2 + tk * n * 2 + tm * n * 4 + tm * n * 4 + tm * n * 2) + 3 * tm * n * 4
    out = pl.pallas_call(
        functools.partial(_mm_res_ln_kernel, nk=nk, has_bias=bias is not None, want_bf16=want_bf16),
        grid=(m // tm, nk),
        in_specs=in_specs,
        out_specs=out_specs,
        out_shape=out_shape,
        scratch_shapes=scratch,
        compiler_params=_params(2, vmem),
        name="mm_res_ln",
    )(*args)
    return (out[0], out[1]) if want_bf16 else (out[0], None)


def _gate_placement():
    pq = np.zeros((N_SPLIT, LANES, LANES), np.float32)
    pk = np.zeros((N_SPLIT, LANES, LANES), np.float32)
    cq = np.zeros((1, LANES), np.float32)
    ck = np.zeros((1, LANES), np.float32)
    dsel = np.zeros((N_HEADS, LANES, LANES), np.float32)
    for h in range(N_HEADS):
        base = GATE_COLS * h
        for p in range(N_SPLIT):
            pq[p, h, base + p] = 1.0
            ck[0, base + p] = 1.0
            pk[p, h, base + N_SPLIT + p] = 1.0
            cq[0, base + N_SPLIT + p] = 1.0
        for cidx in range(GATE_COLS):
            dsel[h, base + cidx, base + cidx] = 1.0
    return (jnp.asarray(pq, BF16), jnp.asarray(pk, BF16), jnp.asarray(cq, F32),
            jnp.asarray(ck, F32), jnp.asarray(dsel, BF16))


def _fox_gate_kernel(h_ref, wf_ref, bf_ref, pq_ref, pk_ref, cq_ref, ck_ref, qx_ref, kx_ref):
    x = jnp.dot(h_ref[...], wf_ref[...], preferred_element_type=F32) + bf_ref[...]
    lf = -(jnp.maximum(-x, 0.0) + jnp.log1p(jnp.exp(-jnp.abs(x))))
    n = lf.shape[0]
    row = lax.broadcasted_iota(jnp.int32, lf.shape, 0)
    c = lf
    sh = 1
    while sh < n:
        c = c + jnp.where(row >= sh, pltpu.roll(c, sh, 0), 0.0)
        sh *= 2
    parts = _split3(c)
    qx = cq_ref[...]
    kx = ck_ref[...]
    for p in range(N_SPLIT):
        qx = qx + jnp.dot(parts[p], pq_ref[p], preferred_element_type=F32)
        kx = kx - jnp.dot(parts[p], pk_ref[p], preferred_element_type=F32)
    qx_ref[...] = qx.astype(BF16)
    kx_ref[...] = kx.astype(BF16)


def _fox_gate(h, wf, bf_row, consts, *, seq):
    m, d = h.shape
    pq, pk, cq, ck, _ = consts
    rows = pl.BlockSpec((seq, LANES), lambda b: (b, 0))
    vec = pl.BlockSpec((1, LANES), lambda b: (0, 0))
    mats = pl.BlockSpec((N_SPLIT, LANES, LANES), lambda b: (0, 0, 0))
    vmem = 2 * seq * d * 2 + 16 * seq * LANES * 4
    return pl.pallas_call(
        _fox_gate_kernel,
        grid=(m // seq,),
        in_specs=[pl.BlockSpec((seq, d), lambda b: (b, 0)),
                  pl.BlockSpec((d, LANES), lambda b: (0, 0)), vec, mats, mats, vec, vec],
        out_specs=[rows, rows],
        out_shape=[jax.ShapeDtypeStruct((m, LANES), BF16)] * 2,
        compiler_params=_params(1, vmem),
        name="fox_gate",
    )(h, wf, bf_row, pq, pk, cq, ck)


def _fox_attn_kernel(q_ref, k_ref, v_ref, qx_ref, kx_ref, dsel_ref, o_ref, kaug_ref, *, tq):
    seq = q_ref.shape[0]
    kaug_ref[:, :HEAD_DIM] = k_ref[...]
    kaug_ref[:, HEAD_DIM:] = jnp.dot(kx_ref[...], dsel_ref[...],
                                     preferred_element_type=F32).astype(BF16)
    contract_last = (((1,), (1,)), ((), ()))
    r = lax.broadcasted_iota(jnp.int32, (tq, tq), 0)
    c = lax.broadcasted_iota(jnp.int32, (tq, tq), 1)
    causal = r >= c
    for lo in range(0, seq, tq):
        rows = slice(lo, lo + tq)
        q_aug = jnp.concatenate([q_ref[rows, :], qx_ref[rows, :]], axis=1)
        s_d = lax.dot_general(q_aug, kaug_ref[rows, :], contract_last, preferred_element_type=F32)
        s_d = jnp.where(causal, s_d, -jnp.inf)
        m = jnp.max(s_d, axis=-1, keepdims=True)
        if lo:
            s_o = lax.dot_general(q_aug, kaug_ref[0:lo, :], contract_last, preferred_element_type=F32)
            m = jnp.maximum(m, jnp.max(s_o, axis=-1, keepdims=True))
            p_o = jnp.exp(s_o - m)
        p_d = jnp.exp(s_d - m)
        l = jnp.sum(p_d, axis=-1, keepdims=True)
        acc = jnp.dot(p_d.astype(BF16), v_ref[rows, :], preferred_element_type=F32)
        if lo:
            l = l + jnp.sum(p_o, axis=-1, keepdims=True)
            acc = acc + jnp.dot(p_o.astype(BF16), v_ref[0:lo, :], preferred_element_type=F32)
        o_ref[rows, :] = (acc / l).astype(o_ref.dtype)


def _fox_attn(qkv, qx, kx, dsel, *, batch, seq, tq=256):
    d_model = N_HEADS * HEAD_DIM
    full = lambda off: pl.BlockSpec((seq, HEAD_DIM), lambda b, h: (b, h + off))
    gate = pl.BlockSpec((seq, LANES), lambda b, h: (b, 0))
    vmem = 2 * 6 * seq * HEAD_DIM * 2 + seq * 2 * HEAD_DIM * 2 + 6 * tq * seq * 4
    return pl.pallas_call(
        functools.partial(_fox_attn_kernel, tq=tq),
        grid=(batch, N_HEADS),
        in_specs=[full(0), full(N_HEADS), full(2 * N_HEADS), gate, gate,
                  pl.BlockSpec((None, LANES, LANES), lambda b, h: (h, 0, 0))],
        out_specs=full(0),
        out_shape=jax.ShapeDtypeStruct((batch * seq, d_model), BF16),
        scratch_shapes=[pltpu.VMEM((seq, 2 * HEAD_DIM), BF16)],
        compiler_params=_params(2, vmem),
        name="fox_attn",
    )(qkv, qkv, qkv, qx, kx, dsel)


def _rel_bias_kernel(rb_ref, o_ref):
    rb = jnp.broadcast_to(rb_ref[...], (8, rb_ref.shape[-1]))
    nrb = rb.shape[1]
    ridx = lax.broadcasted_iota(jnp.int32, (nrb, REL_ROLL_W), 0)
    mcol = lax.broadcasted_iota(jnp.int32, (nrb, REL_ROLL_W), 1)
    off = jnp.where(mcol < REL_TK, mcol, mcol - REL_ROLL_W)
    idx = jnp.clip(LEFT_CHUNKS * CHUNK - off, -REL_CLIP, REL_CLIP) + REL_CLIP
    onehot = jnp.where(ridx == idx, 1.0, 0.0).astype(BF16)
    vec = jnp.zeros((8, REL_ROLL_W), F32)
    for part in _split3(rb):
        vec = vec + jnp.dot(part, onehot, preferred_element_type=F32)
    tile = jnp.broadcast_to(vec[0:1, :], (REL_TQ, REL_ROLL_W))
    tile = pltpu.roll(tile, 0, 1, stride=1, stride_axis=0)[:, :REL_TK]
    qc = jnp.right_shift(lax.broadcasted_iota(jnp.int32, (REL_TQ, REL_TK), 0), 6)
    kc = jnp.right_shift(lax.broadcasted_iota(jnp.int32, (REL_TQ, REL_TK), 1), 6)
    valid = (kc >= qc) & (kc <= qc + LEFT_CHUNKS)
    o_ref[...] = jnp.where(valid, tile, -jnp.inf)


def _rel_bias_tiles(rel_bias):
    h, nb = rel_bias.shape
    nbp = -(-nb // LANES) * LANES
    rb = jnp.pad(rel_bias, ((0, 0), (0, nbp - nb))).reshape(h, 1, nbp)
    vmem = 8 * nbp * REL_ROLL_W * 4 + 8 * REL_TQ * REL_ROLL_W * 4
    return pl.pallas_call(
        _rel_bias_kernel,
        grid=(h,),
        in_specs=[pl.BlockSpec((None, 1, nbp), lambda i: (i, 0, 0))],
        out_specs=pl.BlockSpec((None, REL_TQ, REL_TK), lambda i: (i, 0, 0)),
        out_shape=jax.ShapeDtypeStruct((h, REL_TQ, REL_TK), F32),
        compiler_params=_params(1, vmem),
        name="rel_bias_tiles",
    )(rb)


def _rel_attn_kernel(q_ref, k_ref, v_ref, bias_ref, o_ref):
    seq = q_ref.shape[0]
    for qb in range(seq // REL_TQ):
        rows = slice(qb * REL_TQ, (qb + 1) * REL_TQ)
        keys = slice(max(qb - (REL_KBLOCKS - 1), 0) * REL_TQ, (qb + 1) * REL_TQ)
        width = keys.stop - keys.start
        s = lax.dot_general(q_ref[rows, :], k_ref[keys, :], (((1,), (1,)), ((), ())),
                            preferred_element_type=F32)
        s = s + bias_ref[:, REL_TK - width:]
        m = jnp.max(s, axis=-1, keepdims=True)
        p = jnp.exp(s - m)
        l = jnp.sum(p, axis=-1, keepdims=True)
        o = jnp.dot(p.astype(BF16), v_ref[keys, :], preferred_element_type=F32)
        o_ref[rows, :] = (o / l).astype(o_ref.dtype)


def _rel_attn(qkv, bias_tiles, *, batch, seq):
    d_model = N_HEADS * HEAD_DIM
    assert seq % REL_TQ == 0
    full = lambda off: pl.BlockSpec((seq, HEAD_DIM), lambda h, b: (b, h + off))
    vmem = 2 * (4 * seq * HEAD_DIM * 2 + REL_TQ * REL_TK * 4) + 8 * REL_TQ * REL_TK * 4
    return pl.pallas_call(
        _rel_attn_kernel,
        grid=(N_HEADS, batch),
        in_specs=[full(0), full(N_HEADS), full(2 * N_HEADS),
                  pl.BlockSpec((None, REL_TQ, REL_TK), lambda h, b: (h, 0, 0))],
        out_specs=full(0),
        out_shape=jax.ShapeDtypeStruct((batch * seq, d_model), BF16),
        compiler_params=_params(2, vmem),
        name="rel_attn",
    )(qkv, qkv, qkv, bias_tiles)


def _conv_tail_kernel(u_ref, halo_ref, wdw_ref, bdw_ref, cg_ref, cb_ref, w2_ref, b2_ref,
                      x_ref, g_ref, b_ref, of_ref, ob_ref, ext_ref, y_ref, sh_ref, *, tm, tiles_per_seq):
    i = pl.program_id(0)
    d_model = u_ref.shape[1]
    keep = jnp.where(i % tiles_per_seq == 0, 0.0, 1.0).astype(F32)
    ext_ref[0:CONV_HALO, :] = halo_ref[...] * keep
    ext_ref[CONV_HALO:CONV_HALO + tm, :] = u_ref[...]
    base = CONV_HALO - (CONV_K - 1)
    taps = {r: [(q, 8 * q + r - base) for q in range(CONV_HALO // 8 + 1)
                if 0 <= 8 * q + r - base < CONV_K] for r in range(8)}
    for ci, c0 in enumerate(range(0, d_model, LANES)):
        cols = slice(c0, c0 + LANES)
        buf = ci % 2
        for r in range(1, 8):
            span = 8 * max(q for q, _ in taps[r]) + tm
            sh_ref[buf, r - 1, 0:span, :] = ext_ref[r:r + span, cols]
        acc = jnp.broadcast_to(bdw_ref[:, cols], (tm, LANES))
        for r in range(8):
            for q, k in taps[r]:
                rows = slice(8 * q, 8 * q + tm)
                src = ext_ref[rows, cols] if r == 0 else sh_ref[buf, r - 1, rows, :]
                acc = acc + wdw_ref[k:k + 1, cols] * src
        y_ref[:, cols] = acc
    y = _layer_norm(y_ref[...], cg_ref[...], cb_ref[...])
    y = y * jax.nn.sigmoid(y)
    mix = jnp.dot(y.astype(BF16), w2_ref[...], preferred_element_type=F32) + b2_ref[...]
    out = _layer_norm(ALPHA * x_ref[...] + mix, g_ref[...], b_ref[...])
    of_ref[...] = out
    ob_ref[...] = out.astype(BF16)


def _conv_tail(u, w_dw, b_dw, cg, cb, w2, b2, x_res, ln_g, ln_b, *, seq, tm=256):
    m, d = u.shape
    assert seq % tm == 0 and tm % CONV_HALO == 0
    hb = tm // CONV_HALO
    row = pl.BlockSpec((1, d), lambda i: (0, 0))
    tile = pl.BlockSpec((tm, d), lambda i: (i, 0))
    wdw = jnp.pad(w_dw, ((0, CONV_HALO - CONV_K), (0, 0)))
    vmem = 2 * (tm * d * 4 * 3 + tm * d * 2 + CONV_HALO * d * 8 + d * d * 2) + (2 * tm + CONV_HALO) * d * 4 \
        + 4 * tm * d * 4
    return pl.pallas_call(
        functools.partial(_conv_tail_kernel, tm=tm, tiles_per_seq=seq // tm),
        grid=(m // tm,),
        in_specs=[tile,
                  pl.BlockSpec((CONV_HALO, d), lambda i: (jnp.maximum(i * hb - 1, 0), 0)),
                  pl.BlockSpec((CONV_HALO, d), lambda i: (0, 0)),
                  row, row, row,
                  pl.BlockSpec((d, d), lambda i: (0, 0)),
                  row, tile, row, row],
        out_specs=[tile, tile],
        out_shape=[jax.ShapeDtypeStruct((m, d), F32), jax.ShapeDtypeStruct((m, d), BF16)],
        scratch_shapes=[pltpu.VMEM((CONV_HALO + tm, d), F32), pltpu.VMEM((tm, d), F32),
                        pltpu.VMEM((2, 7, CONV_HALO + tm, LANES), F32)],
        compiler_params=_params(1, vmem),
        name="conv_tail",
    )(u, u, wdw, b_dw, cg, cb, w2, b2, x_res, ln_g, ln_b)


def kernel(x, fox_w_qkv, fox_w_f, fox_b_f, fox_w_o, rel_w_qkv, rel_bias, rel_w_o, conv_w_pw1, conv_b_pw1, conv_w_dw, conv_b_dw, conv_ln_g, conv_ln_b, conv_w_pw2, conv_b_pw2, ffn_w_gate, ffn_w_up, ffn_w_down, ln_mix_g, ln_mix_b, ln_ffn_g, ln_ffn_b):
    batch, seq, d = x.shape
    m = batch * seq
    assert d == N_HEADS * HEAD_DIM
    row = lambda v: v.reshape(1, -1).astype(F32)
    xf = x.reshape(m, d)
    xb = xf.astype(BF16)
    gate_consts = _gate_placement()
    d_ff = ffn_w_gate.shape[2]
    ffn_tk = d_ff // 4
    assert ffn_tk % LANES == 0

    for i in range(DEPTH):
        kind, j = i % N_MIXERS, i // N_MIXERS
        mix_g, mix_b = row(ln_mix_g[i]), row(ln_mix_b[i])
        if kind == 0:
            wf = jnp.pad(fox_w_f[j], ((0, 0), (0, LANES - N_HEADS))).astype(BF16)
            bf = jnp.pad(fox_b_f[j], (0, LANES - N_HEADS)).reshape(1, LANES).astype(F32)
            qkv = _qkv_proj(xb, fox_w_qkv, j)
            qx, kx = _fox_gate(xb, wf, bf, gate_consts, seq=seq)
            o = _fox_attn(qkv, qx, kx, gate_consts[4], batch=batch, seq=seq)
            xf, xb = _mm_res_ln(o, fox_w_o[j].astype(BF16), xf, mix_g, mix_b, tk=d)
        elif kind == 1:
            qkv = _qkv_proj(xb, rel_w_qkv, j)
            tiles = _rel_bias_tiles(rel_bias[j])
            o = _rel_attn(qkv, tiles, batch=batch, seq=seq)
            xf, xb = _mm_res_ln(o, rel_w_o[j].astype(BF16), xf, mix_g, mix_b, tk=d)
        else:
            u = _gated_mm(xb, conv_w_pw1, conv_w_pw1, j, n_out=d, a_off=0, b_off=d, glu=True,
                          bias=row(conv_b_pw1[j]), out_dtype=F32)
            xf, xb = _conv_tail(u, conv_w_dw[j], row(conv_b_dw[j]), row(conv_ln_g[j]), row(conv_ln_b[j]),
                                conv_w_pw2[j].astype(BF16), row(conv_b_pw2[j]), xf, mix_g, mix_b, seq=seq)
        hdn = _gated_mm(xb, ffn_w_gate, ffn_w_up, i, n_out=d_ff, a_off=0, b_off=0, glu=False)
        xf, xb = _mm_res_ln(hdn, ffn_w_down[i].astype(BF16), xf, row(ln_ffn_g[i]), row(ln_ffn_b[i]),
                            want_bf16=i + 1 < DEPTH, tk=ffn_tk)
    return xf.reshape(batch, seq, d)
```

```python
import functools

import numpy as np
import jax
import jax.numpy as jnp
from jax import lax
from jax.experimental import pallas as pl
from jax.experimental.pallas import tpu as pltpu

F32 = jnp.float32
BF16 = jnp.bfloat16

N_HEADS = 16
HEAD_DIM = 128
CHUNK = 64
LEFT_CHUNKS = 8
REL_CLIP = 128
CONV_K = 31
DEPTH = 4
N_MIXERS = 3
LN_EPS = 1e-5
ALPHA = (2.0 * DEPTH) ** 0.25
LOG2E = 1.4426950408889634

LANES = 128
V7X_VMEM_LIMIT = 60 * 1024 * 1024

N_SPLIT = 3
GATE_COLS = 2 * N_SPLIT

REL_TQ = 4 * CHUNK
REL_TK = REL_TQ + LEFT_CHUNKS * CHUNK
REL_KBLOCKS = REL_TK // REL_TQ
REL_ROLL_W = 1024
CONV_HALO = 32


def _params(n_axes, vmem_bytes):
    limit = int(min(max(vmem_bytes * 5 // 4 + (4 << 20), 16 << 20), V7X_VMEM_LIMIT))
    return pltpu.CompilerParams(dimension_semantics=("arbitrary",) * n_axes,
                                vmem_limit_bytes=limit)


def _layer_norm(y, g, b):
    mu = jnp.mean(y, axis=-1, keepdims=True)
    d = y - mu
    var = jnp.mean(d * d, axis=-1, keepdims=True)
    return d * lax.rsqrt(var + LN_EPS) * g + b


def _split3(x):
    hi = x.astype(BF16)
    r = x - hi.astype(F32)
    mid = r.astype(BF16)
    lo = (r - mid.astype(F32)).astype(BF16)
    return hi, mid, lo


def _cast_kernel(x_ref, o_ref):
    o_ref[...] = x_ref[...].astype(o_ref.dtype)


def _to_bf16_rows(w, *, tr=512):
    n = w.shape[-1]
    w2 = w.reshape(-1, n)
    rows = w2.shape[0]
    assert rows % tr == 0
    spec = pl.BlockSpec((tr, n), lambda i: (i, 0))
    return pl.pallas_call(
        _cast_kernel,
        grid=(rows // tr,),
        in_specs=[spec],
        out_specs=spec,
        out_shape=jax.ShapeDtypeStruct((rows, n), BF16),
        compiler_params=_params(1, 2 * tr * n * 6),
        name="cast_bf16",
    )(w2)


def _gated_mm_kernel(*refs, glu, has_bias):
    if has_bias:
        x_ref, wa_ref, wb_ref, ba_ref, bb_ref, o_ref, wa_bf_ref, wb_bf_ref = refs
    else:
        x_ref, wa_ref, wb_ref, o_ref, wa_bf_ref, wb_bf_ref = refs

    @pl.when(pl.program_id(1) == 0)
    def _():
        wa_bf_ref[...] = wa_ref[...].astype(BF16)
        wb_bf_ref[...] = wb_ref[...].astype(BF16)

    x = x_ref[...]
    a = jnp.dot(x, wa_bf_ref[...], preferred_element_type=F32)
    b = jnp.dot(x, wb_bf_ref[...], preferred_element_type=F32)
    if has_bias:
        a = a + ba_ref[...]
        b = b + bb_ref[...]
    if glu:
        out = a * jax.nn.sigmoid(b)
    else:
        out = (a * jax.nn.sigmoid(a)) * b
    o_ref[...] = out.astype(o_ref.dtype)


def _gated_mm(x, wa, wb, layer, *, n_out, a_off, b_off, glu, bias=None, out_dtype=BF16, tm=1024, tn=512):
    m, k = x.shape
    tm = min(tm, m)
    assert m % tm == 0 and n_out % tn == 0 and a_off % tn == 0 and b_off % tn == 0
    in_specs = [
        pl.BlockSpec((tm, k), lambda j, i: (i, 0)),
        pl.BlockSpec((None, k, tn), lambda j, i: (layer, 0, j + a_off // tn)),
        pl.BlockSpec((None, k, tn), lambda j, i: (layer, 0, j + b_off // tn)),
    ]
    args = [x, wa, wb]
    if bias is not None:
        in_specs += [pl.BlockSpec((1, tn), lambda j, i: (0, j + a_off // tn)),
                     pl.BlockSpec((1, tn), lambda j, i: (0, j + b_off // tn))]
        args += [bias, bias]
    vmem = 2 * (tm * k * 2 + 2 * k * tn * 4 + tm * tn * jnp.dtype(out_dtype).itemsize) \
        + 2 * k * tn * 2 + 3 * tm * tn * 4
    return pl.pallas_call(
        functools.partial(_gated_mm_kernel, glu=glu, has_bias=bias is not None),
        grid=(n_out // tn, m // tm),
        in_specs=in_specs,
        out_specs=pl.BlockSpec((tm, tn), lambda j, i: (i, j)),
        out_shape=jax.ShapeDtypeStruct((m, n_out), out_dtype),
        scratch_shapes=[pltpu.VMEM((k, tn), BF16), pltpu.VMEM((k, tn), BF16)],
        compiler_params=_params(2, vmem),
        name="gated_mm_glu" if glu else "gated_mm_swiglu",
    )(*args)


def _qkv_kernel(x_ref, w_ref, o_ref, w_bf_ref, *, q_tiles, scale):
    @pl.when(pl.program_id(1) == 0)
    def _():
        w_bf_ref[...] = w_ref[...].astype(BF16)

    acc = jnp.dot(x_ref[...], w_bf_ref[...], preferred_element_type=F32)
    sc = jnp.where(pl.program_id(0) < q_tiles, scale, 1.0).astype(F32)
    o_ref[...] = (acc * sc).astype(o_ref.dtype)


def _qkv_proj(x, w, layer, *, tm=1024, tn=1024):
    m, k = x.shape
    n = w.shape[2]
    d_model = n // 3
    tm = min(tm, m)
    assert m % tm == 0 and n % tn == 0 and d_model % tn == 0
    vmem = 2 * (tm * k * 2 + k * tn * 4 + tm * tn * 2) + k * tn * 2 + 2 * tm * tn * 4
    return pl.pallas_call(
        functools.partial(_qkv_kernel, q_tiles=d_model // tn, scale=LOG2E * HEAD_DIM ** -0.5),
        grid=(n // tn, m // tm),
        in_specs=[pl.BlockSpec((tm, k), lambda j, i: (i, 0)),
                  pl.BlockSpec((None, k, tn), lambda j, i: (layer, 0, j))],
        out_specs=pl.BlockSpec((tm, tn), lambda j, i: (i, j)),
        out_shape=jax.ShapeDtypeStruct((m, n), BF16),
        scratch_shapes=[pltpu.VMEM((k, tn), BF16)],
        compiler_params=_params(2, vmem),
        name="qkv_proj",
    )(x, w)


LN_ROWS = 128


def _mm_res_ln_kernel(*refs, nm, nj, want_bf16):
    if want_bf16:
        h_ref, w_ref, x_ref, g_ref, b_ref, of_ref, ob_ref, y_ref = refs
    else:
        h_ref, w_ref, x_ref, g_ref, b_ref, of_ref, y_ref = refs
    i = pl.program_id(0)
    j = pl.program_id(1)
    chunk = x_ref.shape[0] // nj

    def matmul():
        y_ref[i % 2, j] = jnp.dot(h_ref[...], w_ref[...], preferred_element_type=F32)

    def norm():
        slot = (i + 1) % 2
        step_rows = pl.ds(pl.multiple_of(j * chunk, chunk), chunk)
        y_slabs = [y_ref[slot, s, step_rows, :] for s in range(nj)]
        for sub in range(0, chunk, LN_ROWS):
            rows = pl.ds(pl.multiple_of(j * chunk + sub, LN_ROWS), LN_ROWS)
            acc = jnp.concatenate([ys[sub:sub + LN_ROWS, :] for ys in y_slabs], axis=1)
            out = _layer_norm(ALPHA * x_ref[rows, :] + acc, g_ref[...], b_ref[...])
            of_ref[rows, :] = out
            if want_bf16:
                ob_ref[rows, :] = out.astype(BF16)

    @pl.when(i == 0)
    def _():
        matmul()

    @pl.when(jnp.logical_and(i > 0, i < nm))
    def _():
        norm()
        matmul()

    @pl.when(i == nm)
    def _():
        norm()


def _mm_res_ln(h, w, layer, x_res, ln_g, ln_b, *, want_bf16=True, tm=512, tn=512):
    m, k = h.shape
    n = w.shape[1]
    tm = min(tm, m)
    nm, nj = m // tm, n // tn
    assert m % tm == 0 and n % tn == 0 and tm % (LN_ROWS * nj) == 0 and w.shape[0] % k == 0
    row = pl.BlockSpec((1, n), lambda i, j: (0, 0))
    lagged = pl.BlockSpec((tm, n), lambda i, j: (jnp.maximum(i - 1, 0), 0))
    in_specs = [pl.BlockSpec((tm, k), lambda i, j: (jnp.minimum(i, nm - 1), 0)),
                pl.BlockSpec((k, tn), lambda i, j: (layer, j)),
                lagged, row, row]
    out_specs = [lagged]
    out_shape = [jax.ShapeDtypeStruct((m, n), F32)]
    if want_bf16:
        out_specs.append(lagged)
        out_shape.append(jax.ShapeDtypeStruct((m, n), BF16))
    vmem = 2 * (tm * k * 2 + k * tn * 2 + 2 * tm * n * 4 + tm * n * 2) + 2 * tm * n * 4 \
        + tm * tn * 4 + 4 * (tm // nj) * n * 4
    out = pl.pallas_call(
        functools.partial(_mm_res_ln_kernel, nm=nm, nj=nj, want_bf16=want_bf16),
        grid=(nm + 1, nj),
        in_specs=in_specs,
        out_specs=out_specs,
        out_shape=out_shape,
        scratch_shapes=[pltpu.VMEM((2, nj, tm, tn), F32)],
        compiler_params=_params(2, vmem),
        name="mm_res_ln",
    )(h, w, x_res, ln_g, ln_b)
    return (out[0], out[1]) if want_bf16 else (out[0], None)


def _gate_placement():
    pq = np.zeros((N_SPLIT, LANES, LANES), np.float32)
    pk = np.zeros((N_SPLIT, LANES, LANES), np.float32)
    cq = np.zeros((1, LANES), np.float32)
    ck = np.zeros((1, LANES), np.float32)
    dsel = np.zeros((N_HEADS, LANES, LANES), np.float32)
    for h in range(N_HEADS):
        base = GATE_COLS * h
        for p in range(N_SPLIT):
            pq[p, h, base + p] = 1.0
            ck[0, base + p] = 1.0
            pk[p, h, base + N_SPLIT + p] = 1.0
            cq[0, base + N_SPLIT + p] = 1.0
        for cidx in range(GATE_COLS):
            dsel[h, base + cidx, base + cidx] = 1.0
    return (jnp.asarray(pq, BF16), jnp.asarray(pk, BF16), jnp.asarray(cq, F32),
            jnp.asarray(ck, F32), jnp.asarray(dsel, BF16))


def _fox_gate_kernel(h_ref, wf_ref, bf_ref, pq_ref, pk_ref, cq_ref, ck_ref, qx_ref, kx_ref):
    x = jnp.dot(h_ref[...], wf_ref[...], preferred_element_type=F32) + bf_ref[...]
    lf = -(jnp.maximum(-x, 0.0) + jnp.log1p(jnp.exp(-jnp.abs(x))))
    n = lf.shape[0]
    row = lax.broadcasted_iota(jnp.int32, lf.shape, 0)
    c = lf
    sh = 1
    while sh < n:
        c = c + jnp.where(row >= sh, pltpu.roll(c, sh, 0), 0.0)
        sh *= 2
    parts = _split3(c * LOG2E)
    qx = cq_ref[...]
    kx = ck_ref[...]
    for p in range(N_SPLIT):
        qx = qx + jnp.dot(parts[p], pq_ref[p], preferred_element_type=F32)
        kx = kx - jnp.dot(parts[p], pk_ref[p], preferred_element_type=F32)
    qx_ref[...] = qx.astype(BF16)
    kx_ref[...] = kx.astype(BF16)


def _fox_gate(h, wf, bf_row, consts, *, seq):
    m, d = h.shape
    pq, pk, cq, ck, _ = consts
    rows = pl.BlockSpec((seq, LANES), lambda b: (b, 0))
    vec = pl.BlockSpec((1, LANES), lambda b: (0, 0))
    mats = pl.BlockSpec((N_SPLIT, LANES, LANES), lambda b: (0, 0, 0))
    vmem = 2 * seq * d * 2 + 16 * seq * LANES * 4
    return pl.pallas_call(
        _fox_gate_kernel,
        grid=(m // seq,),
        in_specs=[pl.BlockSpec((seq, d), lambda b: (b, 0)),
                  pl.BlockSpec((d, LANES), lambda b: (0, 0)), vec, mats, mats, vec, vec],
        out_specs=[rows, rows],
        out_shape=[jax.ShapeDtypeStruct((m, LANES), BF16)] * 2,
        compiler_params=_params(1, vmem),
        name="fox_gate",
    )(h, wf, bf_row, pq, pk, cq, ck)


def _fox_attn_kernel(q_ref, k_ref, v_ref, qx_ref, kx_ref, dsel_ref, o_ref, kaug_ref, *, tq):
    seq = q_ref.shape[0]
    kaug_ref[:, :HEAD_DIM] = k_ref[...]
    kaug_ref[:, HEAD_DIM:] = jnp.dot(kx_ref[...], dsel_ref[...],
                                     preferred_element_type=F32).astype(BF16)
    contract_last = (((1,), (1,)), ((), ()))
    r = lax.broadcasted_iota(jnp.int32, (tq, tq), 0)
    c = lax.broadcasted_iota(jnp.int32, (tq, tq), 1)
    causal = r >= c
    def scores(lo):
        rows = slice(lo, lo + tq)
        q_aug = jnp.concatenate([q_ref[rows, :], qx_ref[rows, :]], axis=1)
        s = lax.dot_general(q_aug, kaug_ref[0:lo + tq, :], contract_last, preferred_element_type=F32)
        diag = jnp.where(causal, s[:, lo:], -jnp.inf)
        return jnp.concatenate([s[:, :lo], diag], axis=1) if lo else diag

    nxt = scores(0)
    for lo in range(0, seq, tq):
        s = nxt
        if lo + tq < seq:
            nxt = scores(lo + tq)
        m = jnp.max(s, axis=-1, keepdims=True)
        p = jnp.exp2(s - m)
        l = jnp.sum(p, axis=-1, keepdims=True)
        acc = jnp.dot(p.astype(BF16), v_ref[0:lo + tq, :], preferred_element_type=F32)
        o_ref[lo:lo + tq, :] = (acc / l).astype(o_ref.dtype)


def _fox_attn(qkv, qx, kx, dsel, *, batch, seq, tq=256):
    d_model = N_HEADS * HEAD_DIM
    full = lambda off: pl.BlockSpec((seq, HEAD_DIM), lambda b, h: (b, h + off))
    gate = pl.BlockSpec((seq, LANES), lambda b, h: (b, 0))
    vmem = 2 * 6 * seq * HEAD_DIM * 2 + seq * 2 * HEAD_DIM * 2 + 6 * tq * seq * 4
    return pl.pallas_call(
        functools.partial(_fox_attn_kernel, tq=tq),
        grid=(batch, N_HEADS),
        in_specs=[full(0), full(N_HEADS), full(2 * N_HEADS), gate, gate,
                  pl.BlockSpec((None, LANES, LANES), lambda b, h: (h, 0, 0))],
        out_specs=full(0),
        out_shape=jax.ShapeDtypeStruct((batch * seq, d_model), BF16),
        scratch_shapes=[pltpu.VMEM((seq, 2 * HEAD_DIM), BF16)],
        compiler_params=_params(2, vmem),
        name="fox_attn",
    )(qkv, qkv, qkv, qx, kx, dsel)


def _rel_bias_kernel(rb_ref, o_ref):
    rb = jnp.broadcast_to(rb_ref[...], (8, rb_ref.shape[-1]))
    nrb = rb.shape[1]
    ridx = lax.broadcasted_iota(jnp.int32, (nrb, REL_ROLL_W), 0)
    mcol = lax.broadcasted_iota(jnp.int32, (nrb, REL_ROLL_W), 1)
    off = jnp.where(mcol < REL_TK, mcol, mcol - REL_ROLL_W)
    idx = jnp.clip(LEFT_CHUNKS * CHUNK - off, -REL_CLIP, REL_CLIP) + REL_CLIP
    onehot = jnp.where(ridx == idx, 1.0, 0.0).astype(BF16)
    vec = jnp.zeros((8, REL_ROLL_W), F32)
    for part in _split3(rb):
        vec = vec + jnp.dot(part, onehot, preferred_element_type=F32)
    tile = jnp.broadcast_to(vec[0:1, :], (REL_TQ, REL_ROLL_W))
    tile = pltpu.roll(tile, 0, 1, stride=1, stride_axis=0)[:, :REL_TK]
    qc = jnp.right_shift(lax.broadcasted_iota(jnp.int32, (REL_TQ, REL_TK), 0), 6)
    kc = jnp.right_shift(lax.broadcasted_iota(jnp.int32, (REL_TQ, REL_TK), 1), 6)
    valid = (kc >= qc) & (kc <= qc + LEFT_CHUNKS)
    o_ref[...] = jnp.where(valid, tile * LOG2E, -jnp.inf)


def _rel_bias_tiles(rel_bias):
    h, nb = rel_bias.shape
    nbp = -(-nb // LANES) * LANES
    rb = jnp.pad(rel_bias, ((0, 0), (0, nbp - nb))).reshape(h, 1, nbp)
    vmem = 8 * nbp * REL_ROLL_W * 4 + 8 * REL_TQ * REL_ROLL_W * 4
    return pl.pallas_call(
        _rel_bias_kernel,
        grid=(h,),
        in_specs=[pl.BlockSpec((None, 1, nbp), lambda i: (i, 0, 0))],
        out_specs=pl.BlockSpec((None, REL_TQ, REL_TK), lambda i: (i, 0, 0)),
        out_shape=jax.ShapeDtypeStruct((h, REL_TQ, REL_TK), F32),
        compiler_params=_params(1, vmem),
        name="rel_bias_tiles",
    )(rb)


def _rel_attn_kernel(q_ref, k_ref, v_ref, bias_ref, o_ref):
    seq = q_ref.shape[0]
    nqb = seq // REL_TQ

    def window(qb):
        return slice(max(qb - (REL_KBLOCKS - 1), 0) * REL_TQ, (qb + 1) * REL_TQ)

    def scores(qb):
        keys = window(qb)
        width = keys.stop - keys.start
        s = lax.dot_general(q_ref[qb * REL_TQ:(qb + 1) * REL_TQ, :], k_ref[keys, :],
                            (((1,), (1,)), ((), ())), preferred_element_type=F32)
        return s + bias_ref[:, REL_TK - width:]

    nxt = scores(0)
    for qb in range(nqb):
        s = nxt
        if qb + 1 < nqb:
            nxt = scores(qb + 1)
        m = jnp.max(s, axis=-1, keepdims=True)
        p = jnp.exp2(s - m)
        l = jnp.sum(p, axis=-1, keepdims=True)
        o = jnp.dot(p.astype(BF16), v_ref[window(qb), :], preferred_element_type=F32)
        o_ref[qb * REL_TQ:(qb + 1) * REL_TQ, :] = (o / l).astype(o_ref.dtype)


def _rel_attn(qkv, bias_tiles, *, batch, seq):
    d_model = N_HEADS * HEAD_DIM
    assert seq % REL_TQ == 0
    full = lambda off: pl.BlockSpec((seq, HEAD_DIM), lambda h, b: (b, h + off))
    vmem = 2 * (4 * seq * HEAD_DIM * 2 + REL_TQ * REL_TK * 4) + 8 * REL_TQ * REL_TK * 4
    return pl.pallas_call(
        _rel_attn_kernel,
        grid=(N_HEADS, batch),
        in_specs=[full(0), full(N_HEADS), full(2 * N_HEADS),
                  pl.BlockSpec((None, REL_TQ, REL_TK), lambda h, b: (h, 0, 0))],
        out_specs=full(0),
        out_shape=jax.ShapeDtypeStruct((batch * seq, d_model), BF16),
        compiler_params=_params(2, vmem),
        name="rel_attn",
    )(qkv, qkv, qkv, bias_tiles)


def _conv_tail_kernel(u_ref, halo_ref, wdw_ref, bdw_ref, cg_ref, cb_ref, w2_ref, b2_ref,
                      x_ref, g_ref, b_ref, of_ref, ob_ref, ext_ref, y_ref, sh_ref, *, tm, tiles_per_seq):
    i = pl.program_id(0)
    d_model = u_ref.shape[1]
    keep = jnp.where(i % tiles_per_seq == 0, 0.0, 1.0).astype(F32)
    ext_ref[0:CONV_HALO, :] = halo_ref[...] * keep
    ext_ref[CONV_HALO:CONV_HALO + tm, :] = u_ref[...]
    base = CONV_HALO - (CONV_K - 1)
    taps = {r: [(q, 8 * q + r - base) for q in range(CONV_HALO // 8 + 1)
                if 0 <= 8 * q + r - base < CONV_K] for r in range(8)}
    for ci, c0 in enumerate(range(0, d_model, LANES)):
        cols = slice(c0, c0 + LANES)
        buf = ci % 2
        for r in range(1, 8):
            span = 8 * max(q for q, _ in taps[r]) + tm
            sh_ref[buf, r - 1, 0:span, :] = ext_ref[r:r + span, cols]
        acc = jnp.broadcast_to(bdw_ref[:, cols], (tm, LANES))
        for r in range(8):
            for q, k in taps[r]:
                rows = slice(8 * q, 8 * q + tm)
                src = ext_ref[rows, cols] if r == 0 else sh_ref[buf, r - 1, rows, :]
                acc = acc + wdw_ref[k:k + 1, cols] * src
        y_ref[:, cols] = acc
    y = _layer_norm(y_ref[...], cg_ref[...], cb_ref[...])
    y = y * jax.nn.sigmoid(y)
    mix = jnp.dot(y.astype(BF16), w2_ref[...], preferred_element_type=F32) + b2_ref[...]
    out = _layer_norm(ALPHA * x_ref[...] + mix, g_ref[...], b_ref[...])
    of_ref[...] = out
    ob_ref[...] = out.astype(BF16)


def _conv_tail(u, w_dw, b_dw, cg, cb, w2, layer, b2, x_res, ln_g, ln_b, *, seq, tm=256):
    m, d = u.shape
    assert seq % tm == 0 and tm % CONV_HALO == 0
    hb = tm // CONV_HALO
    row = pl.BlockSpec((1, d), lambda i: (0, 0))
    tile = pl.BlockSpec((tm, d), lambda i: (i, 0))
    wdw = jnp.pad(w_dw, ((0, CONV_HALO - CONV_K), (0, 0)))
    vmem = 2 * (tm * d * 4 * 3 + tm * d * 2 + CONV_HALO * d * 8 + d * d * 2) + (2 * tm + CONV_HALO) * d * 4 \
        + 4 * tm * d * 4
    return pl.pallas_call(
        functools.partial(_conv_tail_kernel, tm=tm, tiles_per_seq=seq // tm),
        grid=(m // tm,),
        in_specs=[tile,
                  pl.BlockSpec((CONV_HALO, d), lambda i: (jnp.maximum(i * hb - 1, 0), 0)),
                  pl.BlockSpec((CONV_HALO, d), lambda i: (0, 0)),
                  row, row, row,
                  pl.BlockSpec((d, d), lambda i: (layer, 0)),
                  row, tile, row, row],
        out_specs=[tile, tile],
        out_shape=[jax.ShapeDtypeStruct((m, d), F32), jax.ShapeDtypeStruct((m, d), BF16)],
        scratch_shapes=[pltpu.VMEM((CONV_HALO + tm, d), F32), pltpu.VMEM((tm, d), F32),
                        pltpu.VMEM((2, 7, CONV_HALO + tm, LANES), F32)],
        compiler_params=_params(1, vmem),
        name="conv_tail",
    )(u, u, wdw, b_dw, cg, cb, w2, b2, x_res, ln_g, ln_b)


def kernel(x, fox_w_qkv, fox_w_f, fox_b_f, fox_w_o, rel_w_qkv, rel_bias, rel_w_o, conv_w_pw1, conv_b_pw1, conv_w_dw, conv_b_dw, conv_ln_g, conv_ln_b, conv_w_pw2, conv_b_pw2, ffn_w_gate, ffn_w_up, ffn_w_down, ln_mix_g, ln_mix_b, ln_ffn_g, ln_ffn_b):
    batch, seq, d = x.shape
    m = batch * seq
    assert d == N_HEADS * HEAD_DIM
    row = lambda v: v.reshape(1, -1).astype(F32)
    xf = x.reshape(m, d)
    xb = xf.astype(BF16)
    gate_consts = _gate_placement()
    d_ff = ffn_w_gate.shape[2]
    fox_wo, rel_wo, pw2, w_down = (_to_bf16_rows(w) for w in (fox_w_o, rel_w_o, conv_w_pw2, ffn_w_down))

    for i in range(DEPTH):
        kind, j = i % N_MIXERS, i // N_MIXERS
        mix_g, mix_b = row(ln_mix_g[i]), row(ln_mix_b[i])
        if kind == 0:
            wf = jnp.pad(fox_w_f[j], ((0, 0), (0, LANES - N_HEADS))).astype(BF16)
            bf = jnp.pad(fox_b_f[j], (0, LANES - N_HEADS)).reshape(1, LANES).astype(F32)
            qkv = _qkv_proj(xb, fox_w_qkv, j)
            qx, kx = _fox_gate(xb, wf, bf, gate_consts, seq=seq)
            o = _fox_attn(qkv, qx, kx, gate_consts[4], batch=batch, seq=seq)
            xf, xb = _mm_res_ln(o, fox_wo, j, xf, mix_g, mix_b, tn=d // 2)
        elif kind == 1:
            qkv = _qkv_proj(xb, rel_w_qkv, j)
            tiles = _rel_bias_tiles(rel_bias[j])
            o = _rel_attn(qkv, tiles, batch=batch, seq=seq)
            xf, xb = _mm_res_ln(o, rel_wo, j, xf, mix_g, mix_b, tn=d // 2)
        else:
            u = _gated_mm(xb, conv_w_pw1, conv_w_pw1, j, n_out=d, a_off=0, b_off=d, glu=True,
                          bias=row(conv_b_pw1[j]), out_dtype=F32)
            xf, xb = _conv_tail(u, conv_w_dw[j], row(conv_b_dw[j]), row(conv_ln_g[j]), row(conv_ln_b[j]),
                                pw2, j, row(conv_b_pw2[j]), xf, mix_g, mix_b, seq=seq)
        hdn = _gated_mm(xb, ffn_w_gate, ffn_w_up, i, n_out=d_ff, a_off=0, b_off=0, glu=False)
        xf, xb = _mm_res_ln(hdn, w_down, i, xf, row(ln_ffn_g[i]), row(ln_ffn_b[i]),
                            want_bf16=i + 1 < DEPTH)
    return xf.reshape(batch, seq, d)
```

```python
import functools

import numpy as np
import jax
import jax.numpy as jnp
from jax import lax
from jax.experimental import pallas as pl
from jax.experimental.pallas import tpu as pltpu

F32 = jnp.float32
BF16 = jnp.bfloat16

N_HEADS = 16
HEAD_DIM = 128
CHUNK = 64
LEFT_CHUNKS = 8
REL_CLIP = 128
CONV_K = 31
DEPTH = 4
N_MIXERS = 3
LN_EPS = 1e-5
ALPHA = (2.0 * DEPTH) ** 0.25
LOG2E = 1.4426950408889634

LANES = 128
MXU_COLS = 256
V7X_VMEM_LIMIT = 60 * 1024 * 1024

N_SPLIT = 3
GATE_COLS = 2 * N_SPLIT

REL_TQ = 4 * CHUNK
REL_TK = REL_TQ + LEFT_CHUNKS * CHUNK
REL_KBLOCKS = REL_TK // REL_TQ
REL_ROLL_W = 1024
CONV_HALO = 32


def _params(n_axes, vmem_bytes):
    limit = int(min(max(vmem_bytes * 5 // 4 + (4 << 20), 16 << 20), V7X_VMEM_LIMIT))
    return pltpu.CompilerParams(dimension_semantics=("arbitrary",) * n_axes,
                                vmem_limit_bytes=limit)


def _layer_norm(y, g, b):
    mu = jnp.mean(y, axis=-1, keepdims=True)
    d = y - mu
    var = jnp.mean(d * d, axis=-1, keepdims=True)
    return d * lax.rsqrt(var + LN_EPS) * g + b


def _split3(x):
    hi = x.astype(BF16)
    r = x - hi.astype(F32)
    mid = r.astype(BF16)
    lo = (r - mid.astype(F32)).astype(BF16)
    return hi, mid, lo


def _cast_kernel(x_ref, o_ref):
    o_ref[...] = x_ref[...].astype(o_ref.dtype)


def _to_bf16_rows(w, *, tr=512):
    n = w.shape[-1]
    w2 = w.reshape(-1, n)
    rows = w2.shape[0]
    assert rows % tr == 0
    spec = pl.BlockSpec((tr, n), lambda i: (i, 0))
    return pl.pallas_call(
        _cast_kernel,
        grid=(rows // tr,),
        in_specs=[spec],
        out_specs=spec,
        out_shape=jax.ShapeDtypeStruct((rows, n), BF16),
        compiler_params=_params(1, 2 * tr * n * 6),
        name="cast_bf16",
    )(w2)


def _gated_mm_kernel(*refs, glu, has_bias):
    if has_bias:
        x_ref, wa_ref, wb_ref, ba_ref, bb_ref, o_ref, wa_bf_ref, wb_bf_ref = refs
    else:
        x_ref, wa_ref, wb_ref, o_ref, wa_bf_ref, wb_bf_ref = refs

    @pl.when(pl.program_id(1) == 0)
    def _():
        wa_bf_ref[...] = wa_ref[...].astype(BF16)
        wb_bf_ref[...] = wb_ref[...].astype(BF16)

    x = x_ref[...]
    a = jnp.dot(x, wa_bf_ref[...], preferred_element_type=F32)
    b = jnp.dot(x, wb_bf_ref[...], preferred_element_type=F32)
    if has_bias:
        a = a + ba_ref[...]
        b = b + bb_ref[...]
    if glu:
        out = a * jax.nn.sigmoid(b)
    else:
        out = (a * jax.nn.sigmoid(a)) * b
    o_ref[...] = out.astype(o_ref.dtype)


def _gated_mm(x, wa, wb, layer, *, n_out, a_off, b_off, glu, bias=None, out_dtype=BF16, tm=1024, tn=512):
    m, k = x.shape
    tm = min(tm, m)
    assert m % tm == 0 and n_out % tn == 0 and a_off % tn == 0 and b_off % tn == 0
    in_specs = [
        pl.BlockSpec((tm, k), lambda j, i: (i, 0)),
        pl.BlockSpec((None, k, tn), lambda j, i: (layer, 0, j + a_off // tn)),
        pl.BlockSpec((None, k, tn), lambda j, i: (layer, 0, j + b_off // tn)),
    ]
    args = [x, wa, wb]
    if bias is not None:
        in_specs += [pl.BlockSpec((1, tn), lambda j, i: (0, j + a_off // tn)),
                     pl.BlockSpec((1, tn), lambda j, i: (0, j + b_off // tn))]
        args += [bias, bias]
    vmem = 2 * (tm * k * 2 + 2 * k * tn * 4 + tm * tn * jnp.dtype(out_dtype).itemsize) \
        + 2 * k * tn * 2 + 3 * tm * tn * 4
    return pl.pallas_call(
        functools.partial(_gated_mm_kernel, glu=glu, has_bias=bias is not None),
        grid=(n_out // tn, m // tm),
        in_specs=in_specs,
        out_specs=pl.BlockSpec((tm, tn), lambda j, i: (i, j)),
        out_shape=jax.ShapeDtypeStruct((m, n_out), out_dtype),
        scratch_shapes=[pltpu.VMEM((k, tn), BF16), pltpu.VMEM((k, tn), BF16)],
        compiler_params=_params(2, vmem),
        name="gated_mm_glu" if glu else "gated_mm_swiglu",
    )(*args)


def _qkv_kernel(x_ref, w_ref, o_ref, w_bf_ref, *, q_tiles, scale):
    @pl.when(pl.program_id(1) == 0)
    def _():
        w_bf_ref[...] = w_ref[...].astype(BF16)

    acc = jnp.dot(x_ref[...], w_bf_ref[...], preferred_element_type=F32)
    sc = jnp.where(pl.program_id(0) < q_tiles, scale, 1.0).astype(F32)
    o_ref[...] = (acc * sc).astype(o_ref.dtype)


def _qkv_proj(x, w, layer, *, tm=1024, tn=1024):
    m, k = x.shape
    n = w.shape[2]
    d_model = n // 3
    tm = min(tm, m)
    assert m % tm == 0 and n % tn == 0 and d_model % tn == 0
    vmem = 2 * (tm * k * 2 + k * tn * 4 + tm * tn * 2) + k * tn * 2 + 2 * tm * tn * 4
    return pl.pallas_call(
        functools.partial(_qkv_kernel, q_tiles=d_model // tn, scale=LOG2E * HEAD_DIM ** -0.5),
        grid=(n // tn, m // tm),
        in_specs=[pl.BlockSpec((tm, k), lambda j, i: (i, 0)),
                  pl.BlockSpec((None, k, tn), lambda j, i: (layer, 0, j))],
        out_specs=pl.BlockSpec((tm, tn), lambda j, i: (i, j)),
        out_shape=jax.ShapeDtypeStruct((m, n), BF16),
        scratch_shapes=[pltpu.VMEM((k, tn), BF16)],
        compiler_params=_params(2, vmem),
        name="qkv_proj",
    )(x, w)


LN_ROWS = 128


def _mm_res_ln_kernel(*refs, nm, nj, want_bf16):
    if want_bf16:
        h_ref, w_ref, x_ref, g_ref, b_ref, of_ref, ob_ref, y0_ref, y1_ref = refs
    else:
        h_ref, w_ref, x_ref, g_ref, b_ref, of_ref, y0_ref, y1_ref = refs
    i = pl.program_id(0)
    j = pl.program_id(1)
    chunk = x_ref.shape[0] // nj

    def matmul(y_ref):
        y_ref[j] = jnp.dot(h_ref[...], w_ref[...], preferred_element_type=F32)

    def norm(y_ref):
        for sub in range(0, chunk, LN_ROWS):
            rows = pl.ds(pl.multiple_of(j * chunk + sub, LN_ROWS), LN_ROWS)
            acc = jnp.concatenate([y_ref[s, rows, :] for s in range(nj)], axis=1)
            out = _layer_norm(ALPHA * x_ref[rows, :] + acc, g_ref[...], b_ref[...])
            of_ref[rows, :] = out
            if want_bf16:
                ob_ref[rows, :] = out.astype(BF16)

    mid = jnp.logical_and(i > 0, i < nm)

    @pl.when(i == 0)
    def _():
        matmul(y0_ref)

    @pl.when(jnp.logical_and(mid, i % 2 == 1))
    def _():
        matmul(y1_ref)
        norm(y0_ref)

    @pl.when(jnp.logical_and(mid, i % 2 == 0))
    def _():
        matmul(y0_ref)
        norm(y1_ref)

    @pl.when(i == nm)
    def _():
        norm(y1_ref if nm % 2 == 0 else y0_ref)


def _mm_res_ln(h, w, layer, x_res, ln_g, ln_b, *, want_bf16=True, tm=512, tn=512):
    m, k = h.shape
    n = w.shape[1]
    tm = min(tm, m)
    nm, nj = m // tm, n // tn
    assert m % tm == 0 and n % tn == 0 and tm % (LN_ROWS * nj) == 0 and w.shape[0] % k == 0
    row = pl.BlockSpec((1, n), lambda i, j: (0, 0))
    lagged = pl.BlockSpec((tm, n), lambda i, j: (jnp.maximum(i - 1, 0), 0))
    w_mode = dict(pipeline_mode=pl.Buffered(1)) if nj == 1 else {}
    in_specs = [pl.BlockSpec((tm, k), lambda i, j: (jnp.minimum(i, nm - 1), 0)),
                pl.BlockSpec((k, tn), lambda i, j: (layer, j), **w_mode),
                lagged, row, row]
    out_specs = [lagged]
    out_shape = [jax.ShapeDtypeStruct((m, n), F32)]
    if want_bf16:
        out_specs.append(lagged)
        out_shape.append(jax.ShapeDtypeStruct((m, n), BF16))
    vmem = 2 * (tm * k * 2 + 2 * tm * n * 4 + tm * n * 2) + (1 if nj == 1 else 2) * k * tn * 2 \
        + 2 * tm * n * 4 + tm * tn * 4 + 4 * LN_ROWS * n * 4
    out = pl.pallas_call(
        functools.partial(_mm_res_ln_kernel, nm=nm, nj=nj, want_bf16=want_bf16),
        grid=(nm + 1, nj),
        in_specs=in_specs,
        out_specs=out_specs,
        out_shape=out_shape,
        scratch_shapes=[pltpu.VMEM((nj, tm, tn), F32), pltpu.VMEM((nj, tm, tn), F32)],
        compiler_params=_params(2, vmem),
        name="mm_res_ln",
    )(h, w, x_res, ln_g, ln_b)
    return (out[0], out[1]) if want_bf16 else (out[0], None)


def _gate_placement():
    pq = np.zeros((N_SPLIT, LANES, LANES), np.float32)
    pk = np.zeros((N_SPLIT, LANES, LANES), np.float32)
    cq = np.zeros((1, LANES), np.float32)
    ck = np.zeros((1, LANES), np.float32)
    dsel = np.zeros((N_HEADS, LANES, LANES), np.float32)
    for h in range(N_HEADS):
        base = GATE_COLS * h
        for p in range(N_SPLIT):
            pq[p, h, base + p] = 1.0
            ck[0, base + p] = 1.0
            pk[p, h, base + N_SPLIT + p] = 1.0
            cq[0, base + N_SPLIT + p] = 1.0
        for cidx in range(GATE_COLS):
            dsel[h, base + cidx, base + cidx] = 1.0
    return (jnp.asarray(pq, BF16), jnp.asarray(pk, BF16), jnp.asarray(cq, F32),
            jnp.asarray(ck, F32), jnp.asarray(dsel, BF16))


def _fox_gate_kernel(h_ref, wf_ref, bf_ref, pq_ref, pk_ref, cq_ref, ck_ref, qx_ref, kx_ref):
    x = jnp.dot(h_ref[...], wf_ref[...], preferred_element_type=F32) + bf_ref[...]
    lf = -(jnp.maximum(-x, 0.0) + jnp.log1p(jnp.exp(-jnp.abs(x))))
    n = lf.shape[0]
    row = lax.broadcasted_iota(jnp.int32, lf.shape, 0)
    c = lf
    sh = 1
    while sh < n:
        c = c + jnp.where(row >= sh, pltpu.roll(c, sh, 0), 0.0)
        sh *= 2
    parts = _split3(c * LOG2E)
    qx = cq_ref[...]
    kx = ck_ref[...]
    for p in range(N_SPLIT):
        qx = qx + jnp.dot(parts[p], pq_ref[p], preferred_element_type=F32)
        kx = kx - jnp.dot(parts[p], pk_ref[p], preferred_element_type=F32)
    qx_ref[...] = qx.astype(BF16)
    kx_ref[...] = kx.astype(BF16)


def _fox_gate(h, wf, bf_row, consts, *, seq):
    m, d = h.shape
    pq, pk, cq, ck, _ = consts
    rows = pl.BlockSpec((seq, LANES), lambda b: (b, 0))
    vec = pl.BlockSpec((1, LANES), lambda b: (0, 0))
    mats = pl.BlockSpec((N_SPLIT, LANES, LANES), lambda b: (0, 0, 0))
    vmem = 2 * seq * d * 2 + 16 * seq * LANES * 4
    return pl.pallas_call(
        _fox_gate_kernel,
        grid=(m // seq,),
        in_specs=[pl.BlockSpec((seq, d), lambda b: (b, 0)),
                  pl.BlockSpec((d, LANES), lambda b: (0, 0)), vec, mats, mats, vec, vec],
        out_specs=[rows, rows],
        out_shape=[jax.ShapeDtypeStruct((m, LANES), BF16)] * 2,
        compiler_params=_params(1, vmem),
        name="fox_gate",
    )(h, wf, bf_row, pq, pk, cq, ck)


def _fox_attn_kernel(q_ref, k_ref, v_ref, qx_ref, kx_ref, dsel_ref, o_ref, kaug_ref, *, tq):
    seq = q_ref.shape[0]
    kaug_ref[:, :HEAD_DIM] = k_ref[...]
    kaug_ref[:, HEAD_DIM:] = jnp.dot(kx_ref[...], dsel_ref[...],
                                     preferred_element_type=F32).astype(BF16)
    contract_last = (((1,), (1,)), ((), ()))
    r = lax.broadcasted_iota(jnp.int32, (tq, tq), 0)
    c = lax.broadcasted_iota(jnp.int32, (tq, tq), 1)
    causal = r >= c
    def scores(lo):
        rows = slice(lo, lo + tq)
        q_aug = jnp.concatenate([q_ref[rows, :], qx_ref[rows, :]], axis=1)
        s = lax.dot_general(q_aug, kaug_ref[0:lo + tq, :], contract_last, preferred_element_type=F32)
        diag = jnp.where(causal, s[:, lo:], -jnp.inf)
        return jnp.concatenate([s[:, :lo], diag], axis=1) if lo else diag

    nxt = scores(0)
    for lo in range(0, seq, tq):
        s = nxt
        if lo + tq < seq:
            nxt = scores(lo + tq)
        m = jnp.max(s, axis=-1, keepdims=True)
        p = jnp.exp2(s - m)
        l = jnp.sum(p, axis=-1, keepdims=True)
        acc = jnp.dot(p.astype(BF16), v_ref[0:lo + tq, :], preferred_element_type=F32)
        o_ref[lo:lo + tq, :] = (acc / l).astype(o_ref.dtype)


def _fox_attn(qkv, qx, kx, dsel, *, batch, seq, tq=256):
    d_model = N_HEADS * HEAD_DIM
    full = lambda off: pl.BlockSpec((seq, HEAD_DIM), lambda b, h: (b, h + off))
    gate = pl.BlockSpec((seq, LANES), lambda b, h: (b, 0))
    vmem = 2 * 6 * seq * HEAD_DIM * 2 + seq * 2 * HEAD_DIM * 2 + 6 * tq * seq * 4
    return pl.pallas_call(
        functools.partial(_fox_attn_kernel, tq=tq),
        grid=(batch, N_HEADS),
        in_specs=[full(0), full(N_HEADS), full(2 * N_HEADS), gate, gate,
                  pl.BlockSpec((None, LANES, LANES), lambda b, h: (h, 0, 0))],
        out_specs=full(0),
        out_shape=jax.ShapeDtypeStruct((batch * seq, d_model), BF16),
        scratch_shapes=[pltpu.VMEM((seq, 2 * HEAD_DIM), BF16)],
        compiler_params=_params(2, vmem),
        name="fox_attn",
    )(qkv, qkv, qkv, qx, kx, dsel)


def _rel_bias_kernel(rb_ref, o_ref):
    rb = jnp.broadcast_to(rb_ref[...], (8, rb_ref.shape[-1]))
    nrb = rb.shape[1]
    ridx = lax.broadcasted_iota(jnp.int32, (nrb, REL_ROLL_W), 0)
    mcol = lax.broadcasted_iota(jnp.int32, (nrb, REL_ROLL_W), 1)
    off = jnp.where(mcol < REL_TK, mcol, mcol - REL_ROLL_W)
    idx = jnp.clip(LEFT_CHUNKS * CHUNK - off, -REL_CLIP, REL_CLIP) + REL_CLIP
    onehot = jnp.where(ridx == idx, 1.0, 0.0).astype(BF16)
    vec = jnp.zeros((8, REL_ROLL_W), F32)
    for part in _split3(rb):
        vec = vec + jnp.dot(part, onehot, preferred_element_type=F32)
    tile = jnp.broadcast_to(vec[0:1, :], (REL_TQ, REL_ROLL_W))
    tile = pltpu.roll(tile, 0, 1, stride=1, stride_axis=0)[:, :REL_TK]
    qc = jnp.right_shift(lax.broadcasted_iota(jnp.int32, (REL_TQ, REL_TK), 0), 6)
    kc = jnp.right_shift(lax.broadcasted_iota(jnp.int32, (REL_TQ, REL_TK), 1), 6)
    valid = (kc >= qc) & (kc <= qc + LEFT_CHUNKS)
    o_ref[...] = jnp.where(valid, tile * LOG2E, -jnp.inf)


def _rel_bias_tiles(rel_bias):
    h, nb = rel_bias.shape
    nbp = -(-nb // LANES) * LANES
    rb = jnp.pad(rel_bias, ((0, 0), (0, nbp - nb))).reshape(h, 1, nbp)
    vmem = 8 * nbp * REL_ROLL_W * 4 + 8 * REL_TQ * REL_ROLL_W * 4
    return pl.pallas_call(
        _rel_bias_kernel,
        grid=(h,),
        in_specs=[pl.BlockSpec((None, 1, nbp), lambda i: (i, 0, 0))],
        out_specs=pl.BlockSpec((None, REL_TQ, REL_TK), lambda i: (i, 0, 0)),
        out_shape=jax.ShapeDtypeStruct((h, REL_TQ, REL_TK), F32),
        compiler_params=_params(1, vmem),
        name="rel_bias_tiles",
    )(rb)


def _rel_attn_kernel(q_ref, k_ref, v_ref, bias_ref, o_ref):
    seq = q_ref.shape[0]
    nqb = seq // REL_TQ

    def window(qb):
        return slice(max(qb - (REL_KBLOCKS - 1), 0) * REL_TQ, (qb + 1) * REL_TQ)

    def scores(qb):
        keys = window(qb)
        width = keys.stop - keys.start
        s = lax.dot_general(q_ref[qb * REL_TQ:(qb + 1) * REL_TQ, :], k_ref[keys, :],
                            (((1,), (1,)), ((), ())), preferred_element_type=F32)
        return s + bias_ref[:, REL_TK - width:]

    nxt = scores(0)
    for qb in range(nqb):
        s = nxt
        if qb + 1 < nqb:
            nxt = scores(qb + 1)
        m = jnp.max(s, axis=-1, keepdims=True)
        p = jnp.exp2(s - m)
        l = jnp.sum(p, axis=-1, keepdims=True)
        o = jnp.dot(p.astype(BF16), v_ref[window(qb), :], preferred_element_type=F32)
        o_ref[qb * REL_TQ:(qb + 1) * REL_TQ, :] = (o / l).astype(o_ref.dtype)


def _rel_attn(qkv, bias_tiles, *, batch, seq):
    d_model = N_HEADS * HEAD_DIM
    assert seq % REL_TQ == 0
    full = lambda off: pl.BlockSpec((seq, HEAD_DIM), lambda h, b: (b, h + off))
    vmem = 2 * (4 * seq * HEAD_DIM * 2 + REL_TQ * REL_TK * 4) + 8 * REL_TQ * REL_TK * 4
    return pl.pallas_call(
        _rel_attn_kernel,
        grid=(N_HEADS, batch),
        in_specs=[full(0), full(N_HEADS), full(2 * N_HEADS),
                  pl.BlockSpec((None, REL_TQ, REL_TK), lambda h, b: (h, 0, 0))],
        out_specs=full(0),
        out_shape=jax.ShapeDtypeStruct((batch * seq, d_model), BF16),
        compiler_params=_params(2, vmem),
        name="rel_attn",
    )(qkv, qkv, qkv, bias_tiles)


def _conv_tail_kernel(u_ref, halo_ref, wdw_ref, bdw_ref, cg_ref, cb_ref, w2_ref, b2_ref,
                      x_ref, g_ref, b_ref, of_ref, ob_ref, ext_ref, y0_ref, y1_ref, mix_ref, sh_ref,
                      *, tm, nt, tiles_per_seq):
    i = pl.program_id(0)
    d_model = u_ref.shape[1]
    base = CONV_HALO - (CONV_K - 1)
    taps = {r: [(q, 8 * q + r - base) for q in range(CONV_HALO // 8 + 1)
                if 0 <= 8 * q + r - base < CONV_K] for r in range(8)}

    n_chunks = d_model // LANES
    n_slabs = d_model // MXU_COLS
    chunks_per_slab = n_chunks // n_slabs

    def conv_stage():
        keep = jnp.where(i % tiles_per_seq == 0, 0.0, 1.0).astype(F32)
        ext_ref[0:CONV_HALO, :] = halo_ref[...] * keep
        ext_ref[CONV_HALO:CONV_HALO + tm, :] = u_ref[...]

    def conv_chunk(y_ref, ci):
        cols = slice(ci * LANES, (ci + 1) * LANES)
        buf = ci % 2
        for r in range(1, 8):
            span = 8 * max(q for q, _ in taps[r]) + tm
            sh_ref[buf, r - 1, 0:span, :] = ext_ref[r:r + span, cols]
        acc = jnp.broadcast_to(bdw_ref[:, cols], (tm, LANES))
        for r in range(8):
            for q, k in taps[r]:
                rows = slice(8 * q, 8 * q + tm)
                src = ext_ref[rows, cols] if r == 0 else sh_ref[buf, r - 1, rows, :]
                acc = acc + wdw_ref[k:k + 1, cols] * src
        y_ref[:, cols] = acc

    def activated(y_ref):
        y = _layer_norm(y_ref[...], cg_ref[...], cb_ref[...])
        return (y * jax.nn.sigmoid(y)).astype(BF16)

    def pointwise_slab(act, s):
        cols = slice(s * MXU_COLS, (s + 1) * MXU_COLS)
        mix_ref[:, cols] = jnp.dot(act, w2_ref[:, cols], preferred_element_type=F32) + b2_ref[:, cols]

    def finish():
        out = _layer_norm(ALPHA * x_ref[...] + mix_ref[...], g_ref[...], b_ref[...])
        of_ref[...] = out
        ob_ref[...] = out.astype(BF16)

    def conv(y_ref):
        conv_stage()
        for ci in range(n_chunks):
            conv_chunk(y_ref, ci)

    def tail(y_ref):
        act = activated(y_ref)
        for s in range(n_slabs):
            pointwise_slab(act, s)
        finish()

    def both(y_prev_ref, y_cur_ref):
        act = activated(y_prev_ref)
        conv_stage()
        for s in range(n_slabs):
            pointwise_slab(act, s)
            for ci in range(s * chunks_per_slab, (s + 1) * chunks_per_slab):
                conv_chunk(y_cur_ref, ci)
        finish()

    mid = jnp.logical_and(i > 0, i < nt)

    @pl.when(i == 0)
    def _():
        conv(y0_ref)

    @pl.when(jnp.logical_and(mid, i % 2 == 1))
    def _():
        both(y0_ref, y1_ref)

    @pl.when(jnp.logical_and(mid, i % 2 == 0))
    def _():
        both(y1_ref, y0_ref)

    @pl.when(i == nt)
    def _():
        tail(y1_ref if nt % 2 == 0 else y0_ref)


def _conv_tail(u, w_dw, b_dw, cg, cb, w2, layer, b2, x_res, ln_g, ln_b, *, seq, tm=256):
    m, d = u.shape
    assert seq % tm == 0 and tm % CONV_HALO == 0
    hb = tm // CONV_HALO
    nt = m // tm
    row = pl.BlockSpec((1, d), lambda i: (0, 0))
    cur = lambda i: jnp.minimum(i, nt - 1)
    lagged = pl.BlockSpec((tm, d), lambda i: (jnp.maximum(i - 1, 0), 0))
    wdw = jnp.pad(w_dw, ((0, CONV_HALO - CONV_K), (0, 0)))
    vmem = 2 * (tm * d * 4 * 3 + tm * d * 2 + CONV_HALO * d * 8) + d * d * 2 + (3 * tm + CONV_HALO) * d * 4 \
        + 4 * tm * d * 4
    return pl.pallas_call(
        functools.partial(_conv_tail_kernel, tm=tm, nt=nt, tiles_per_seq=seq // tm),
        grid=(nt + 1,),
        in_specs=[pl.BlockSpec((tm, d), lambda i: (cur(i), 0)),
                  pl.BlockSpec((CONV_HALO, d), lambda i: (jnp.maximum(cur(i) * hb - 1, 0), 0)),
                  pl.BlockSpec((CONV_HALO, d), lambda i: (0, 0)),
                  row, row, row,
                  pl.BlockSpec((d, d), lambda i: (layer, 0), pipeline_mode=pl.Buffered(1)),
                  row, lagged, row, row],
        out_specs=[lagged, lagged],
        out_shape=[jax.ShapeDtypeStruct((m, d), F32), jax.ShapeDtypeStruct((m, d), BF16)],
        scratch_shapes=[pltpu.VMEM((CONV_HALO + tm, d), F32), pltpu.VMEM((tm, d), F32),
                        pltpu.VMEM((tm, d), F32), pltpu.VMEM((tm, d), F32),
                        pltpu.VMEM((2, 7, CONV_HALO + tm, LANES), F32)],
        compiler_params=_params(1, vmem),
        name="conv_tail",
    )(u, u, wdw, b_dw, cg, cb, w2, b2, x_res, ln_g, ln_b)


def kernel(x, fox_w_qkv, fox_w_f, fox_b_f, fox_w_o, rel_w_qkv, rel_bias, rel_w_o, conv_w_pw1, conv_b_pw1, conv_w_dw, conv_b_dw, conv_ln_g, conv_ln_b, conv_w_pw2, conv_b_pw2, ffn_w_gate, ffn_w_up, ffn_w_down, ln_mix_g, ln_mix_b, ln_ffn_g, ln_ffn_b):
    batch, seq, d = x.shape
    m = batch * seq
    assert d == N_HEADS * HEAD_DIM
    row = lambda v: v.reshape(1, -1).astype(F32)
    xf = x.reshape(m, d)
    xb = xf.astype(BF16)
    gate_consts = _gate_placement()
    d_ff = ffn_w_gate.shape[2]
    fox_wo, rel_wo, pw2, w_down = (_to_bf16_rows(w) for w in (fox_w_o, rel_w_o, conv_w_pw2, ffn_w_down))

    for i in range(DEPTH):
        kind, j = i % N_MIXERS, i // N_MIXERS
        mix_g, mix_b = row(ln_mix_g[i]), row(ln_mix_b[i])
        if kind == 0:
            wf = jnp.pad(fox_w_f[j], ((0, 0), (0, LANES - N_HEADS))).astype(BF16)
            bf = jnp.pad(fox_b_f[j], (0, LANES - N_HEADS)).reshape(1, LANES).astype(F32)
            qkv = _qkv_proj(xb, fox_w_qkv, j)
            qx, kx = _fox_gate(xb, wf, bf, gate_consts, seq=seq)
            o = _fox_attn(qkv, qx, kx, gate_consts[4], batch=batch, seq=seq)
            xf, xb = _mm_res_ln(o, fox_wo, j, xf, mix_g, mix_b, tn=d)
        elif kind == 1:
            qkv = _qkv_proj(xb, rel_w_qkv, j)
            tiles = _rel_bias_tiles(rel_bias[j])
            o = _rel_attn(qkv, tiles, batch=batch, seq=seq)
            xf, xb = _mm_res_ln(o, rel_wo, j, xf, mix_g, mix_b, tn=d)
        else:
            u = _gated_mm(xb, conv_w_pw1, conv_w_pw1, j, n_out=d, a_off=0, b_off=d, glu=True,
                          bias=row(conv_b_pw1[j]), out_dtype=F32)
            xf, xb = _conv_tail(u, conv_w_dw[j], row(conv_b_dw[j]), row(conv_ln_g[j]), row(conv_ln_b[j]),
                                pw2, j, row(conv_b_pw2[j]), xf, mix_g, mix_b, seq=seq)
        hdn = _gated_mm(xb, ffn_w_gate, ffn_w_up, i, n_out=d_ff, a_off=0, b_off=0, glu=False)
        xf, xb = _mm_res_ln(hdn, w_down, i, xf, row(ln_ffn_g[i]), row(ln_ffn_b[i]),
                            want_bf16=i + 1 < DEPTH, tm=256, tn=d)
    return xf.reshape(batch, seq, d)
```

```python
import functools

import numpy as np
import jax
import jax.numpy as jnp
from jax import lax
from jax.experimental import pallas as pl
from jax.experimental.pallas import tpu as pltpu

F32 = jnp.float32
BF16 = jnp.bfloat16

N_HEADS = 16
HEAD_DIM = 128
CHUNK = 64
LEFT_CHUNKS = 8
REL_CLIP = 128
CONV_K = 31
DEPTH = 4
N_MIXERS = 3
LN_EPS = 1e-5
ALPHA = (2.0 * DEPTH) ** 0.25
LOG2E = 1.4426950408889634

LANES = 128
MXU_COLS = 256
BF16_ROWS = 16
V7X_VMEM_LIMIT = 60 * 1024 * 1024

N_SPLIT = 3
GATE_COLS = 2 * N_SPLIT

REL_TQ = 4 * CHUNK
REL_TK = REL_TQ + LEFT_CHUNKS * CHUNK
REL_KBLOCKS = REL_TK // REL_TQ
REL_ROLL_W = 1024
CONV_HALO = 32


def _params(n_axes, vmem_bytes):
    limit = int(min(max(vmem_bytes * 5 // 4 + (4 << 20), 16 << 20), V7X_VMEM_LIMIT))
    return pltpu.CompilerParams(dimension_semantics=("arbitrary",) * n_axes,
                                vmem_limit_bytes=limit)


def _layer_norm(y, g, b):
    mu = jnp.mean(y, axis=-1, keepdims=True)
    d = y - mu
    var = jnp.mean(d * d, axis=-1, keepdims=True)
    return d * lax.rsqrt(var + LN_EPS) * g + b


def _split3(x):
    hi = x.astype(BF16)
    r = x - hi.astype(F32)
    mid = r.astype(BF16)
    lo = (r - mid.astype(F32)).astype(BF16)
    return hi, mid, lo


def _cast_kernel(x_ref, o_ref):
    o_ref[...] = x_ref[...].astype(o_ref.dtype)


def _to_bf16_rows(w, *, tr=512):
    n = w.shape[-1]
    w2 = w.reshape(-1, n)
    rows = w2.shape[0]
    assert rows % tr == 0
    spec = pl.BlockSpec((tr, n), lambda i: (i, 0))
    return pl.pallas_call(
        _cast_kernel,
        grid=(rows // tr,),
        in_specs=[spec],
        out_specs=spec,
        out_shape=jax.ShapeDtypeStruct((rows, n), BF16),
        compiler_params=_params(1, 2 * tr * n * 6),
        name="cast_bf16",
    )(w2)


def _gated_mm_kernel(*refs, glu, has_bias, has_rider):
    refs = list(refs)
    x_ref, wa_ref, wb_ref = refs[:3]
    del refs[:3]
    if has_bias:
        ba_ref, bb_ref = refs[:2]
        del refs[:2]
    if has_rider:
        rider_in_ref, o_ref, rider_out_ref, wa_bf_ref, wb_bf_ref = refs
        rider_out_ref[...] = rider_in_ref[...].astype(BF16)
    else:
        o_ref, wa_bf_ref, wb_bf_ref = refs

    @pl.when(pl.program_id(1) == 0)
    def _():
        wa_bf_ref[...] = wa_ref[...].astype(BF16)
        wb_bf_ref[...] = wb_ref[...].astype(BF16)

    x = x_ref[...]
    a = jnp.dot(x, wa_bf_ref[...], preferred_element_type=F32)
    b = jnp.dot(x, wb_bf_ref[...], preferred_element_type=F32)
    if has_bias:
        a = a + ba_ref[...]
        b = b + bb_ref[...]
    if glu:
        out = a * jax.nn.sigmoid(b)
    else:
        out = (a * jax.nn.sigmoid(a)) * b
    o_ref[...] = out.astype(o_ref.dtype)


def _gated_mm(x, wa, wb, layer, *, n_out, a_off, b_off, glu, bias=None, rider=None, out_dtype=BF16,
              tm=1024, tn=512):
    m, k = x.shape
    tm = min(tm, m)
    assert m % tm == 0 and n_out % tn == 0 and a_off % tn == 0 and b_off % tn == 0
    nj, nm = n_out // tn, m // tm
    in_specs = [
        pl.BlockSpec((tm, k), lambda j, i: (i, 0)),
        pl.BlockSpec((None, k, tn), lambda j, i: (layer, 0, j + a_off // tn)),
        pl.BlockSpec((None, k, tn), lambda j, i: (layer, 0, j + b_off // tn)),
    ]
    args = [x, wa, wb]
    if bias is not None:
        in_specs += [pl.BlockSpec((1, tn), lambda j, i: (0, j + a_off // tn)),
                     pl.BlockSpec((1, tn), lambda j, i: (0, j + b_off // tn))]
        args += [bias, bias]
    out_specs = [pl.BlockSpec((tm, tn), lambda j, i: (i, j))]
    out_shape = [jax.ShapeDtypeStruct((m, n_out), out_dtype)]
    vmem = 2 * (tm * k * 2 + 2 * k * tn * 4 + tm * tn * jnp.dtype(out_dtype).itemsize) \
        + 2 * k * tn * 2 + 3 * tm * tn * 4
    if rider is not None:
        _, r_rows, r_cols = rider.shape
        slab = r_rows // (nj * nm)
        assert r_rows % (nj * nm) == 0 and slab % BF16_ROWS == 0
        in_specs.append(pl.BlockSpec((None, slab, r_cols), lambda j, i: (layer, j * nm + i, 0)))
        args.append(rider)
        out_specs.append(pl.BlockSpec((slab, r_cols), lambda j, i: (j * nm + i, 0)))
        out_shape.append(jax.ShapeDtypeStruct((r_rows, r_cols), BF16))
        vmem += 2 * slab * r_cols * 6
    out = pl.pallas_call(
        functools.partial(_gated_mm_kernel, glu=glu, has_bias=bias is not None, has_rider=rider is not None),
        grid=(nj, nm),
        in_specs=in_specs,
        out_specs=out_specs,
        out_shape=out_shape,
        scratch_shapes=[pltpu.VMEM((k, tn), BF16), pltpu.VMEM((k, tn), BF16)],
        compiler_params=_params(2, vmem),
        name="gated_mm_glu" if glu else "gated_mm_swiglu",
    )(*args)
    return (out[0], out[1]) if rider is not None else out[0]


def _qkv_kernel(x_ref, w_ref, o_ref, w_bf_ref, *, q_tiles, scale):
    @pl.when(pl.program_id(1) == 0)
    def _():
        w_bf_ref[...] = w_ref[...].astype(BF16)

    acc = jnp.dot(x_ref[...], w_bf_ref[...], preferred_element_type=F32)
    sc = jnp.where(pl.program_id(0) < q_tiles, scale, 1.0).astype(F32)
    o_ref[...] = (acc * sc).astype(o_ref.dtype)


def _qkv_proj(x, w, layer, *, tm=2048, tn=1024):
    m, k = x.shape
    n = w.shape[2]
    d_model = n // 3
    tm = min(tm, m)
    assert m % tm == 0 and n % tn == 0 and d_model % tn == 0
    vmem = 2 * (tm * k * 2 + k * tn * 4 + tm * tn * 2) + k * tn * 2 + 2 * tm * tn * 4
    return pl.pallas_call(
        functools.partial(_qkv_kernel, q_tiles=d_model // tn, scale=LOG2E * HEAD_DIM ** -0.5),
        grid=(n // tn, m // tm),
        in_specs=[pl.BlockSpec((tm, k), lambda j, i: (i, 0)),
                  pl.BlockSpec((None, k, tn), lambda j, i: (layer, 0, j))],
        out_specs=pl.BlockSpec((tm, tn), lambda j, i: (i, j)),
        out_shape=jax.ShapeDtypeStruct((m, n), BF16),
        scratch_shapes=[pltpu.VMEM((k, tn), BF16)],
        compiler_params=_params(2, vmem),
        name="qkv_proj",
    )(x, w)


LN_ROWS = 128


def _mm_res_ln_kernel(*refs, nm, nj, want_bf16):
    if want_bf16:
        h_ref, w_ref, x_ref, g_ref, b_ref, of_ref, ob_ref, y0_ref, y1_ref = refs
    else:
        h_ref, w_ref, x_ref, g_ref, b_ref, of_ref, y0_ref, y1_ref = refs
    i = pl.program_id(0)
    j = pl.program_id(1)
    chunk = x_ref.shape[0] // nj

    def matmul(y_ref):
        y_ref[j] = jnp.dot(h_ref[...], w_ref[...], preferred_element_type=F32)

    def norm(y_ref):
        for sub in range(0, chunk, LN_ROWS):
            rows = pl.ds(pl.multiple_of(j * chunk + sub, LN_ROWS), LN_ROWS)
            acc = jnp.concatenate([y_ref[s, rows, :] for s in range(nj)], axis=1)
            out = _layer_norm(ALPHA * x_ref[rows, :] + acc, g_ref[...], b_ref[...])
            of_ref[rows, :] = out
            if want_bf16:
                ob_ref[rows, :] = out.astype(BF16)

    mid = jnp.logical_and(i > 0, i < nm)

    @pl.when(i == 0)
    def _():
        matmul(y0_ref)

    @pl.when(jnp.logical_and(mid, i % 2 == 1))
    def _():
        matmul(y1_ref)
        norm(y0_ref)

    @pl.when(jnp.logical_and(mid, i % 2 == 0))
    def _():
        matmul(y0_ref)
        norm(y1_ref)

    @pl.when(i == nm)
    def _():
        norm(y1_ref if nm % 2 == 0 else y0_ref)


def _mm_res_ln(h, w, layer, x_res, ln_g, ln_b, *, want_bf16=True, tm=512, tn=512):
    m, k = h.shape
    n = w.shape[1]
    tm = min(tm, m)
    nm, nj = m // tm, n // tn
    assert m % tm == 0 and n % tn == 0 and tm % (LN_ROWS * nj) == 0 and w.shape[0] % k == 0
    row = pl.BlockSpec((1, n), lambda i, j: (0, 0))
    lagged = pl.BlockSpec((tm, n), lambda i, j: (jnp.maximum(i - 1, 0), 0))
    w_mode = dict(pipeline_mode=pl.Buffered(1)) if nj == 1 else {}
    in_specs = [pl.BlockSpec((tm, k), lambda i, j: (jnp.minimum(i, nm - 1), 0)),
                pl.BlockSpec((k, tn), lambda i, j: (layer, j), **w_mode),
                lagged, row, row]
    out_specs = [lagged]
    out_shape = [jax.ShapeDtypeStruct((m, n), F32)]
    if want_bf16:
        out_specs.append(lagged)
        out_shape.append(jax.ShapeDtypeStruct((m, n), BF16))
    vmem = 2 * (tm * k * 2 + 2 * tm * n * 4 + tm * n * 2) + (1 if nj == 1 else 2) * k * tn * 2 \
        + 2 * tm * n * 4 + tm * tn * 4 + 4 * LN_ROWS * n * 4
    out = pl.pallas_call(
        functools.partial(_mm_res_ln_kernel, nm=nm, nj=nj, want_bf16=want_bf16),
        grid=(nm + 1, nj),
        in_specs=in_specs,
        out_specs=out_specs,
        out_shape=out_shape,
        scratch_shapes=[pltpu.VMEM((nj, tm, tn), F32), pltpu.VMEM((nj, tm, tn), F32)],
        compiler_params=_params(2, vmem),
        name="mm_res_ln",
    )(h, w, x_res, ln_g, ln_b)
    return (out[0], out[1]) if want_bf16 else (out[0], None)


def _gate_placement():
    pq = np.zeros((N_SPLIT, LANES, LANES), np.float32)
    pk = np.zeros((N_SPLIT, LANES, LANES), np.float32)
    cq = np.zeros((1, LANES), np.float32)
    ck = np.zeros((1, LANES), np.float32)
    hsel = np.zeros((N_HEADS, BF16_ROWS, LANES), np.float32)
    for h in range(N_HEADS):
        base = GATE_COLS * h
        for p in range(N_SPLIT):
            pq[p, h, base + p] = 1.0
            ck[0, base + p] = 1.0
            pk[p, h, base + N_SPLIT + p] = 1.0
            cq[0, base + N_SPLIT + p] = 1.0
        hsel[h, :, base:base + GATE_COLS] = 1.0
    return (jnp.asarray(pq, BF16), jnp.asarray(pk, BF16), jnp.asarray(cq, F32),
            jnp.asarray(ck, F32), jnp.asarray(hsel, BF16))


def _fox_gate_kernel(h_ref, wf_ref, bf_ref, pq_ref, pk_ref, cq_ref, ck_ref, qx_ref, kx_ref):
    x = jnp.dot(h_ref[...], wf_ref[...], preferred_element_type=F32) + bf_ref[...]
    lf = -(jnp.maximum(-x, 0.0) + jnp.log1p(jnp.exp(-jnp.abs(x))))
    n = lf.shape[0]
    row = lax.broadcasted_iota(jnp.int32, lf.shape, 0)
    c = lf
    sh = 1
    while sh < n:
        c = c + jnp.where(row >= sh, pltpu.roll(c, sh, 0), 0.0)
        sh *= 2
    parts = _split3(c * LOG2E)
    qx = cq_ref[...]
    kx = ck_ref[...]
    for p in range(N_SPLIT):
        qx = qx + jnp.dot(parts[p], pq_ref[p], preferred_element_type=F32)
        kx = kx - jnp.dot(parts[p], pk_ref[p], preferred_element_type=F32)
    qx_ref[...] = qx.astype(BF16)
    kx_ref[...] = kx.astype(BF16)


def _fox_gate(h, wf, bf_row, consts, *, seq):
    m, d = h.shape
    pq, pk, cq, ck, _ = consts
    rows = pl.BlockSpec((seq, LANES), lambda b: (b, 0))
    vec = pl.BlockSpec((1, LANES), lambda b: (0, 0))
    mats = pl.BlockSpec((N_SPLIT, LANES, LANES), lambda b: (0, 0, 0))
    vmem = 2 * seq * d * 2 + 16 * seq * LANES * 4
    return pl.pallas_call(
        _fox_gate_kernel,
        grid=(m // seq,),
        in_specs=[pl.BlockSpec((seq, d), lambda b: (b, 0)),
                  pl.BlockSpec((d, LANES), lambda b: (0, 0)), vec, mats, mats, vec, vec],
        out_specs=[rows, rows],
        out_shape=[jax.ShapeDtypeStruct((m, LANES), BF16)] * 2,
        compiler_params=_params(1, vmem),
        name="fox_gate",
    )(h, wf, bf_row, pq, pk, cq, ck)


def _fox_attn_kernel(q_ref, k_ref, v_ref, qx_ref, kx_ref, hsel_ref, o_ref, kaug_ref, *, tq):
    seq = q_ref.shape[0]
    kaug_ref[:, :HEAD_DIM] = k_ref[...]
    kaug_ref[:, HEAD_DIM:] = kx_ref[...] * hsel_ref[0:1, :]
    contract_last = (((1,), (1,)), ((), ()))
    r = lax.broadcasted_iota(jnp.int32, (tq, tq), 0)
    c = lax.broadcasted_iota(jnp.int32, (tq, tq), 1)
    causal = r >= c
    def scores(lo):
        rows = slice(lo, lo + tq)
        q_aug = jnp.concatenate([q_ref[rows, :], qx_ref[rows, :]], axis=1)
        s = lax.dot_general(q_aug, kaug_ref[0:lo + tq, :], contract_last, preferred_element_type=F32)
        diag = jnp.where(causal, s[:, lo:], -jnp.inf)
        return jnp.concatenate([s[:, :lo], diag], axis=1) if lo else diag

    nxt = scores(0)
    for lo in range(0, seq, tq):
        s = nxt
        if lo + tq < seq:
            nxt = scores(lo + tq)
        m = jnp.max(s, axis=-1, keepdims=True)
        p = jnp.exp2(s - m)
        l = jnp.sum(p, axis=-1, keepdims=True)
        acc = jnp.dot(p.astype(BF16), v_ref[0:lo + tq, :], preferred_element_type=F32)
        o_ref[lo:lo + tq, :] = (acc / l).astype(o_ref.dtype)


def _fox_attn(qkv, qx, kx, hsel, *, batch, seq, tq=256):
    d_model = N_HEADS * HEAD_DIM
    full = lambda off: pl.BlockSpec((seq, HEAD_DIM), lambda b, h: (b, h + off))
    gate = pl.BlockSpec((seq, LANES), lambda b, h: (b, 0))
    vmem = 2 * 6 * seq * HEAD_DIM * 2 + seq * 2 * HEAD_DIM * 2 + 6 * tq * seq * 4
    return pl.pallas_call(
        functools.partial(_fox_attn_kernel, tq=tq),
        grid=(batch, N_HEADS),
        in_specs=[full(0), full(N_HEADS), full(2 * N_HEADS), gate, gate,
                  pl.BlockSpec((None, BF16_ROWS, LANES), lambda b, h: (h, 0, 0))],
        out_specs=full(0),
        out_shape=jax.ShapeDtypeStruct((batch * seq, d_model), BF16),
        scratch_shapes=[pltpu.VMEM((seq, 2 * HEAD_DIM), BF16)],
        compiler_params=_params(2, vmem),
        name="fox_attn",
    )(qkv, qkv, qkv, qx, kx, hsel)


def _rel_bias_kernel(rb_ref, o_ref):
    rb = jnp.broadcast_to(rb_ref[...], (8, rb_ref.shape[-1]))
    nrb = rb.shape[1]
    ridx = lax.broadcasted_iota(jnp.int32, (nrb, REL_ROLL_W), 0)
    mcol = lax.broadcasted_iota(jnp.int32, (nrb, REL_ROLL_W), 1)
    off = jnp.where(mcol < REL_TK, mcol, mcol - REL_ROLL_W)
    idx = jnp.clip(LEFT_CHUNKS * CHUNK - off, -REL_CLIP, REL_CLIP) + REL_CLIP
    onehot = jnp.where(ridx == idx, 1.0, 0.0).astype(BF16)
    vec = jnp.zeros((8, REL_ROLL_W), F32)
    for part in _split3(rb):
        vec = vec + jnp.dot(part, onehot, preferred_element_type=F32)
    tile = jnp.broadcast_to(vec[0:1, :], (REL_TQ, REL_ROLL_W))
    tile = pltpu.roll(tile, 0, 1, stride=1, stride_axis=0)[:, :REL_TK]
    qc = jnp.right_shift(lax.broadcasted_iota(jnp.int32, (REL_TQ, REL_TK), 0), 6)
    kc = jnp.right_shift(lax.broadcasted_iota(jnp.int32, (REL_TQ, REL_TK), 1), 6)
    valid = (kc >= qc) & (kc <= qc + LEFT_CHUNKS)
    o_ref[...] = jnp.where(valid, tile * LOG2E, -jnp.inf)


def _rel_bias_tiles(rel_bias):
    h, nb = rel_bias.shape
    nbp = -(-nb // LANES) * LANES
    rb = jnp.pad(rel_bias, ((0, 0), (0, nbp - nb))).reshape(h, 1, nbp)
    vmem = 8 * nbp * REL_ROLL_W * 4 + 8 * REL_TQ * REL_ROLL_W * 4
    return pl.pallas_call(
        _rel_bias_kernel,
        grid=(h,),
        in_specs=[pl.BlockSpec((None, 1, nbp), lambda i: (i, 0, 0))],
        out_specs=pl.BlockSpec((None, REL_TQ, REL_TK), lambda i: (i, 0, 0)),
        out_shape=jax.ShapeDtypeStruct((h, REL_TQ, REL_TK), F32),
        compiler_params=_params(1, vmem),
        name="rel_bias_tiles",
    )(rb)


def _rel_attn_kernel(q_ref, k_ref, v_ref, bias_ref, o_ref):
    seq = q_ref.shape[0]
    nqb = seq // REL_TQ

    def window(qb):
        return slice(max(qb - (REL_KBLOCKS - 1), 0) * REL_TQ, (qb + 1) * REL_TQ)

    def scores(qb):
        keys = window(qb)
        width = keys.stop - keys.start
        s = lax.dot_general(q_ref[qb * REL_TQ:(qb + 1) * REL_TQ, :], k_ref[keys, :],
                            (((1,), (1,)), ((), ())), preferred_element_type=F32)
        return s + bias_ref[:, REL_TK - width:]

    nxt = scores(0)
    for qb in range(nqb):
        s = nxt
        if qb + 1 < nqb:
            nxt = scores(qb + 1)
        m = jnp.max(s, axis=-1, keepdims=True)
        p = jnp.exp2(s - m)
        l = jnp.sum(p, axis=-1, keepdims=True)
        o = jnp.dot(p.astype(BF16), v_ref[window(qb), :], preferred_element_type=F32)
        o_ref[qb * REL_TQ:(qb + 1) * REL_TQ, :] = (o / l).astype(o_ref.dtype)


def _rel_attn(qkv, bias_tiles, *, batch, seq):
    d_model = N_HEADS * HEAD_DIM
    assert seq % REL_TQ == 0
    full = lambda off: pl.BlockSpec((seq, HEAD_DIM), lambda h, b: (b, h + off))
    vmem = 2 * (4 * seq * HEAD_DIM * 2 + REL_TQ * REL_TK * 4) + 8 * REL_TQ * REL_TK * 4
    return pl.pallas_call(
        _rel_attn_kernel,
        grid=(N_HEADS, batch),
        in_specs=[full(0), full(N_HEADS), full(2 * N_HEADS),
                  pl.BlockSpec((None, REL_TQ, REL_TK), lambda h, b: (h, 0, 0))],
        out_specs=full(0),
        out_shape=jax.ShapeDtypeStruct((batch * seq, d_model), BF16),
        compiler_params=_params(2, vmem),
        name="rel_attn",
    )(qkv, qkv, qkv, bias_tiles)


def _conv_tail_kernel(u_ref, halo_ref, wdw_ref, bdw_ref, cg_ref, cb_ref, w2_ref, b2_ref,
                      x_ref, g_ref, b_ref, of_ref, ob_ref, ext_ref, y0_ref, y1_ref, mix_ref, sh_ref,
                      *, tm, nt, tiles_per_seq):
    i = pl.program_id(0)
    d_model = u_ref.shape[1]
    base = CONV_HALO - (CONV_K - 1)
    taps = {r: [(q, 8 * q + r - base) for q in range(CONV_HALO // 8 + 1)
                if 0 <= 8 * q + r - base < CONV_K] for r in range(8)}

    n_chunks = d_model // LANES
    n_slabs = d_model // MXU_COLS
    chunks_per_slab = n_chunks // n_slabs

    def conv_stage():
        keep = jnp.where(i % tiles_per_seq == 0, 0.0, 1.0).astype(F32)
        ext_ref[0:CONV_HALO, :] = halo_ref[...] * keep
        ext_ref[CONV_HALO:CONV_HALO + tm, :] = u_ref[...]

    def conv_chunk(y_ref, ci):
        cols = slice(ci * LANES, (ci + 1) * LANES)
        buf = ci % 2
        for r in range(1, 8):
            span = 8 * max(q for q, _ in taps[r]) + tm
            sh_ref[buf, r - 1, 0:span, :] = ext_ref[r:r + span, cols]
        acc = jnp.broadcast_to(bdw_ref[:, cols], (tm, LANES))
        for r in range(8):
            for q, k in taps[r]:
                rows = slice(8 * q, 8 * q + tm)
                src = ext_ref[rows, cols] if r == 0 else sh_ref[buf, r - 1, rows, :]
                acc = acc + wdw_ref[k:k + 1, cols] * src
        y_ref[:, cols] = acc

    def activated(y_ref):
        y = _layer_norm(y_ref[...], cg_ref[...], cb_ref[...])
        return (y * jax.nn.sigmoid(y)).astype(BF16)

    def pointwise_slab(act, s):
        cols = slice(s * MXU_COLS, (s + 1) * MXU_COLS)
        mix_ref[:, cols] = jnp.dot(act, w2_ref[:, cols], preferred_element_type=F32) + b2_ref[:, cols]

    def finish():
        out = _layer_norm(ALPHA * x_ref[...] + mix_ref[...], g_ref[...], b_ref[...])
        of_ref[...] = out
        ob_ref[...] = out.astype(BF16)

    def conv(y_ref):
        conv_stage()
        for ci in range(n_chunks):
            conv_chunk(y_ref, ci)

    def tail(y_ref):
        act = activated(y_ref)
        for s in range(n_slabs):
            pointwise_slab(act, s)
        finish()

    def both(y_prev_ref, y_cur_ref):
        act = activated(y_prev_ref)
        conv_stage()
        for s in range(n_slabs):
            pointwise_slab(act, s)
            for ci in range(s * chunks_per_slab, (s + 1) * chunks_per_slab):
                conv_chunk(y_cur_ref, ci)
        finish()

    mid = jnp.logical_and(i > 0, i < nt)

    @pl.when(i == 0)
    def _():
        conv(y0_ref)

    @pl.when(jnp.logical_and(mid, i % 2 == 1))
    def _():
        both(y0_ref, y1_ref)

    @pl.when(jnp.logical_and(mid, i % 2 == 0))
    def _():
        both(y1_ref, y0_ref)

    @pl.when(i == nt)
    def _():
        tail(y1_ref if nt % 2 == 0 else y0_ref)


def _conv_tail(u, w_dw, b_dw, cg, cb, w2, layer, b2, x_res, ln_g, ln_b, *, seq, tm=256):
    m, d = u.shape
    assert seq % tm == 0 and tm % CONV_HALO == 0
    hb = tm // CONV_HALO
    nt = m // tm
    row = pl.BlockSpec((1, d), lambda i: (0, 0))
    cur = lambda i: jnp.minimum(i, nt - 1)
    lagged = pl.BlockSpec((tm, d), lambda i: (jnp.maximum(i - 1, 0), 0))
    wdw = jnp.pad(w_dw, ((0, CONV_HALO - CONV_K), (0, 0)))
    vmem = 2 * (tm * d * 4 * 3 + tm * d * 2 + CONV_HALO * d * 8) + d * d * 2 + (3 * tm + CONV_HALO) * d * 4 \
        + 4 * tm * d * 4
    return pl.pallas_call(
        functools.partial(_conv_tail_kernel, tm=tm, nt=nt, tiles_per_seq=seq // tm),
        grid=(nt + 1,),
        in_specs=[pl.BlockSpec((tm, d), lambda i: (cur(i), 0)),
                  pl.BlockSpec((CONV_HALO, d), lambda i: (jnp.maximum(cur(i) * hb - 1, 0), 0)),
                  pl.BlockSpec((CONV_HALO, d), lambda i: (0, 0)),
                  row, row, row,
                  pl.BlockSpec((d, d), lambda i: (layer, 0), pipeline_mode=pl.Buffered(1)),
                  row, lagged, row, row],
        out_specs=[lagged, lagged],
        out_shape=[jax.ShapeDtypeStruct((m, d), F32), jax.ShapeDtypeStruct((m, d), BF16)],
        scratch_shapes=[pltpu.VMEM((CONV_HALO + tm, d), F32), pltpu.VMEM((tm, d), F32),
                        pltpu.VMEM((tm, d), F32), pltpu.VMEM((tm, d), F32),
                        pltpu.VMEM((2, 7, CONV_HALO + tm, LANES), F32)],
        compiler_params=_params(1, vmem),
        name="conv_tail",
    )(u, u, wdw, b_dw, cg, cb, w2, b2, x_res, ln_g, ln_b)


def kernel(x, fox_w_qkv, fox_w_f, fox_b_f, fox_w_o, rel_w_qkv, rel_bias, rel_w_o, conv_w_pw1, conv_b_pw1, conv_w_dw, conv_b_dw, conv_ln_g, conv_ln_b, conv_w_pw2, conv_b_pw2, ffn_w_gate, ffn_w_up, ffn_w_down, ln_mix_g, ln_mix_b, ln_ffn_g, ln_ffn_b):
    batch, seq, d = x.shape
    m = batch * seq
    assert d == N_HEADS * HEAD_DIM
    row = lambda v: v.reshape(1, -1).astype(F32)
    xf = x.reshape(m, d)
    xb = xf.astype(BF16)
    gate_consts = _gate_placement()
    d_ff = ffn_w_gate.shape[2]
    fox_wo, rel_wo, pw2 = (_to_bf16_rows(w) for w in (fox_w_o, rel_w_o, conv_w_pw2))

    for i in range(DEPTH):
        kind, j = i % N_MIXERS, i // N_MIXERS
        mix_g, mix_b = row(ln_mix_g[i]), row(ln_mix_b[i])
        if kind == 0:
            wf = jnp.pad(fox_w_f[j], ((0, 0), (0, LANES - N_HEADS))).astype(BF16)
            bf = jnp.pad(fox_b_f[j], (0, LANES - N_HEADS)).reshape(1, LANES).astype(F32)
            qkv = _qkv_proj(xb, fox_w_qkv, j)
            qx, kx = _fox_gate(xb, wf, bf, gate_consts, seq=seq)
            o = _fox_attn(qkv, qx, kx, gate_consts[4], batch=batch, seq=seq)
            xf, xb = _mm_res_ln(o, fox_wo, j, xf, mix_g, mix_b, tn=d)
        elif kind == 1:
            qkv = _qkv_proj(xb, rel_w_qkv, j)
            tiles = _rel_bias_tiles(rel_bias[j])
            o = _rel_attn(qkv, tiles, batch=batch, seq=seq)
            xf, xb = _mm_res_ln(o, rel_wo, j, xf, mix_g, mix_b, tn=d)
        else:
            u = _gated_mm(xb, conv_w_pw1, conv_w_pw1, j, n_out=d, a_off=0, b_off=d, glu=True,
                          bias=row(conv_b_pw1[j]), out_dtype=F32)
            xf, xb = _conv_tail(u, conv_w_dw[j], row(conv_b_dw[j]), row(conv_ln_g[j]), row(conv_ln_b[j]),
                                pw2, j, row(conv_b_pw2[j]), xf, mix_g, mix_b, seq=seq)
        hdn, w_down = _gated_mm(xb, ffn_w_gate, ffn_w_up, i, n_out=d_ff, a_off=0, b_off=0, glu=False,
                                rider=ffn_w_down)
        xf, xb = _mm_res_ln(hdn, w_down, 0, xf, row(ln_ffn_g[i]), row(ln_ffn_b[i]),
                            want_bf16=i + 1 < DEPTH, tm=256, tn=d)
    return xf.reshape(batch, seq, d)
```

```python
import functools

import numpy as np
import jax
import jax.numpy as jnp
from jax import lax
from jax.experimental import pallas as pl
from jax.experimental.pallas import tpu as pltpu

F32 = jnp.float32
BF16 = jnp.bfloat16

N_HEADS = 16
HEAD_DIM = 128
CHUNK = 64
LEFT_CHUNKS = 8
REL_CLIP = 128
CONV_K = 31
DEPTH = 4
N_MIXERS = 3
LN_EPS = 1e-5
ALPHA = (2.0 * DEPTH) ** 0.25
LOG2E = 1.4426950408889634

LANES = 128
MXU_COLS = 256
BF16_ROWS = 16
V7X_VMEM_LIMIT = 60 * 1024 * 1024

N_SPLIT = 3
GATE_COLS = 2 * N_SPLIT

REL_TQ = 4 * CHUNK
REL_TK = REL_TQ + LEFT_CHUNKS * CHUNK
REL_KBLOCKS = REL_TK // REL_TQ
REL_ROLL_W = 1024
CONV_HALO = 32


def _params(n_axes, vmem_bytes):
    limit = int(min(max(vmem_bytes * 5 // 4 + (4 << 20), 16 << 20), V7X_VMEM_LIMIT))
    return pltpu.CompilerParams(dimension_semantics=("arbitrary",) * n_axes,
                                vmem_limit_bytes=limit)


def _layer_norm(y, g, b):
    mu = jnp.mean(y, axis=-1, keepdims=True)
    d = y - mu
    var = jnp.mean(d * d, axis=-1, keepdims=True)
    return d * lax.rsqrt(var + LN_EPS) * g + b


def _split3(x):
    hi = x.astype(BF16)
    r = x - hi.astype(F32)
    mid = r.astype(BF16)
    lo = (r - mid.astype(F32)).astype(BF16)
    return hi, mid, lo


def _cast_kernel(x_ref, o_ref):
    o_ref[...] = x_ref[...].astype(o_ref.dtype)


def _to_bf16_rows(w, *, tr=512):
    n = w.shape[-1]
    w2 = w.reshape(-1, n)
    rows = w2.shape[0]
    assert rows % tr == 0
    spec = pl.BlockSpec((tr, n), lambda i: (i, 0))
    return pl.pallas_call(
        _cast_kernel,
        grid=(rows // tr,),
        in_specs=[spec],
        out_specs=spec,
        out_shape=jax.ShapeDtypeStruct((rows, n), BF16),
        compiler_params=_params(1, 2 * tr * n * 6),
        name="cast_bf16",
    )(w2)


def _gated_mm_kernel(*refs, glu, has_bias, has_rider):
    refs = list(refs)
    x_ref, wa_ref, wb_ref = refs[:3]
    del refs[:3]
    if has_bias:
        ba_ref, bb_ref = refs[:2]
        del refs[:2]
    if has_rider:
        rider_in_ref, o_ref, rider_out_ref, wa_bf_ref, wb_bf_ref = refs
        rider_out_ref[...] = rider_in_ref[...].astype(BF16)
    else:
        o_ref, wa_bf_ref, wb_bf_ref = refs

    @pl.when(pl.program_id(1) == 0)
    def _():
        wa_bf_ref[...] = wa_ref[...].astype(BF16)
        wb_bf_ref[...] = wb_ref[...].astype(BF16)

    x = x_ref[...]
    a = jnp.dot(x, wa_bf_ref[...], preferred_element_type=F32)
    b = jnp.dot(x, wb_bf_ref[...], preferred_element_type=F32)
    if has_bias:
        a = a + ba_ref[...]
        b = b + bb_ref[...]
    if glu:
        out = a * jax.nn.sigmoid(b)
    else:
        out = (a * jax.nn.sigmoid(a)) * b
    o_ref[...] = out.astype(o_ref.dtype)


def _gated_mm(x, wa, wb, layer, *, n_out, a_off, b_off, glu, bias=None, rider=None, out_dtype=BF16,
              tm=1024, tn=512):
    m, k = x.shape
    tm = min(tm, m)
    assert m % tm == 0 and n_out % tn == 0 and a_off % tn == 0 and b_off % tn == 0
    nj, nm = n_out // tn, m // tm
    in_specs = [
        pl.BlockSpec((tm, k), lambda j, i: (i, 0)),
        pl.BlockSpec((None, k, tn), lambda j, i: (layer, 0, j + a_off // tn)),
        pl.BlockSpec((None, k, tn), lambda j, i: (layer, 0, j + b_off // tn)),
    ]
    args = [x, wa, wb]
    if bias is not None:
        in_specs += [pl.BlockSpec((1, tn), lambda j, i: (0, j + a_off // tn)),
                     pl.BlockSpec((1, tn), lambda j, i: (0, j + b_off // tn))]
        args += [bias, bias]
    out_specs = [pl.BlockSpec((tm, tn), lambda j, i: (i, j))]
    out_shape = [jax.ShapeDtypeStruct((m, n_out), out_dtype)]
    vmem = 2 * (tm * k * 2 + 2 * k * tn * 4 + tm * tn * jnp.dtype(out_dtype).itemsize) \
        + 2 * k * tn * 2 + 3 * tm * tn * 4
    if rider is not None:
        _, r_rows, r_cols = rider.shape
        slab = r_rows // (nj * nm)
        assert r_rows % (nj * nm) == 0 and slab % BF16_ROWS == 0
        in_specs.append(pl.BlockSpec((None, slab, r_cols), lambda j, i: (layer, j * nm + i, 0)))
        args.append(rider)
        out_specs.append(pl.BlockSpec((slab, r_cols), lambda j, i: (j * nm + i, 0)))
        out_shape.append(jax.ShapeDtypeStruct((r_rows, r_cols), BF16))
        vmem += 2 * slab * r_cols * 6
    out = pl.pallas_call(
        functools.partial(_gated_mm_kernel, glu=glu, has_bias=bias is not None, has_rider=rider is not None),
        grid=(nj, nm),
        in_specs=in_specs,
        out_specs=out_specs,
        out_shape=out_shape,
        scratch_shapes=[pltpu.VMEM((k, tn), BF16), pltpu.VMEM((k, tn), BF16)],
        compiler_params=_params(2, vmem),
        name="gated_mm_glu" if glu else "gated_mm_swiglu",
    )(*args)
    return (out[0], out[1]) if rider is not None else out[0]


def _qkv_kernel(x_ref, w_ref, o_ref, w_bf_ref, *, q_tiles, scale):
    @pl.when(pl.program_id(1) == 0)
    def _():
        w_bf_ref[...] = w_ref[...].astype(BF16)

    acc = jnp.dot(x_ref[...], w_bf_ref[...], preferred_element_type=F32)
    sc = jnp.where(pl.program_id(0) < q_tiles, scale, 1.0).astype(F32)
    o_ref[...] = (acc * sc).astype(o_ref.dtype)


def _qkv_proj(x, w, layer, *, tm=2048, tn=1024):
    m, k = x.shape
    n = w.shape[2]
    d_model = n // 3
    tm = min(tm, m)
    assert m % tm == 0 and n % tn == 0 and d_model % tn == 0
    vmem = 2 * (tm * k * 2 + k * tn * 4 + tm * tn * 2) + k * tn * 2 + 2 * tm * tn * 4
    return pl.pallas_call(
        functools.partial(_qkv_kernel, q_tiles=d_model // tn, scale=LOG2E * HEAD_DIM ** -0.5),
        grid=(n // tn, m // tm),
        in_specs=[pl.BlockSpec((tm, k), lambda j, i: (i, 0)),
                  pl.BlockSpec((None, k, tn), lambda j, i: (layer, 0, j))],
        out_specs=pl.BlockSpec((tm, tn), lambda j, i: (i, j)),
        out_shape=jax.ShapeDtypeStruct((m, n), BF16),
        scratch_shapes=[pltpu.VMEM((k, tn), BF16)],
        compiler_params=_params(2, vmem),
        name="qkv_proj",
    )(x, w)


LN_ROWS = 128


def _mm_res_ln_kernel(*refs, nm, nj, want_bf16):
    if want_bf16:
        h_ref, w_ref, x_ref, g_ref, b_ref, of_ref, ob_ref, y0_ref, y1_ref = refs
    else:
        h_ref, w_ref, x_ref, g_ref, b_ref, of_ref, y0_ref, y1_ref = refs
    i = pl.program_id(0)
    j = pl.program_id(1)
    chunk = x_ref.shape[0] // nj

    def matmul(y_ref):
        y_ref[j] = jnp.dot(h_ref[...], w_ref[...], preferred_element_type=F32)

    def norm(y_ref):
        for sub in range(0, chunk, LN_ROWS):
            rows = pl.ds(pl.multiple_of(j * chunk + sub, LN_ROWS), LN_ROWS)
            acc = jnp.concatenate([y_ref[s, rows, :] for s in range(nj)], axis=1)
            out = _layer_norm(ALPHA * x_ref[rows, :] + acc, g_ref[...], b_ref[...])
            of_ref[rows, :] = out
            if want_bf16:
                ob_ref[rows, :] = out.astype(BF16)

    mid = jnp.logical_and(i > 0, i < nm)

    @pl.when(i == 0)
    def _():
        matmul(y0_ref)

    @pl.when(jnp.logical_and(mid, i % 2 == 1))
    def _():
        matmul(y1_ref)
        norm(y0_ref)

    @pl.when(jnp.logical_and(mid, i % 2 == 0))
    def _():
        matmul(y0_ref)
        norm(y1_ref)

    @pl.when(i == nm)
    def _():
        norm(y1_ref if nm % 2 == 0 else y0_ref)


def _mm_res_ln(h, w, layer, x_res, ln_g, ln_b, *, want_bf16=True, tm=512, tn=512):
    m, k = h.shape
    n = w.shape[1]
    tm = min(tm, m)
    nm, nj = m // tm, n // tn
    assert m % tm == 0 and n % tn == 0 and tm % (LN_ROWS * nj) == 0 and w.shape[0] % k == 0
    row = pl.BlockSpec((1, n), lambda i, j: (0, 0))
    lagged = pl.BlockSpec((tm, n), lambda i, j: (jnp.maximum(i - 1, 0), 0))
    w_mode = dict(pipeline_mode=pl.Buffered(1)) if nj == 1 else {}
    in_specs = [pl.BlockSpec((tm, k), lambda i, j: (jnp.minimum(i, nm - 1), 0)),
                pl.BlockSpec((k, tn), lambda i, j: (layer, j), **w_mode),
                lagged, row, row]
    out_specs = [lagged]
    out_shape = [jax.ShapeDtypeStruct((m, n), F32)]
    if want_bf16:
        out_specs.append(lagged)
        out_shape.append(jax.ShapeDtypeStruct((m, n), BF16))
    vmem = 2 * (tm * k * 2 + 2 * tm * n * 4 + tm * n * 2) + (1 if nj == 1 else 2) * k * tn * 2 \
        + 2 * tm * n * 4 + tm * tn * 4 + 4 * LN_ROWS * n * 4
    out = pl.pallas_call(
        functools.partial(_mm_res_ln_kernel, nm=nm, nj=nj, want_bf16=want_bf16),
        grid=(nm + 1, nj),
        in_specs=in_specs,
        out_specs=out_specs,
        out_shape=out_shape,
        scratch_shapes=[pltpu.VMEM((nj, tm, tn), F32), pltpu.VMEM((nj, tm, tn), F32)],
        compiler_params=_params(2, vmem),
        name="mm_res_ln",
    )(h, w, x_res, ln_g, ln_b)
    return (out[0], out[1]) if want_bf16 else (out[0], None)


def _gate_placement():
    pq = np.zeros((N_SPLIT, LANES, LANES), np.float32)
    pk = np.zeros((N_SPLIT, LANES, LANES), np.float32)
    cq = np.zeros((1, LANES), np.float32)
    ck = np.zeros((1, LANES), np.float32)
    hsel = np.zeros((N_HEADS, BF16_ROWS, LANES), np.float32)
    for h in range(N_HEADS):
        base = GATE_COLS * h
        for p in range(N_SPLIT):
            pq[p, h, base + p] = 1.0
            ck[0, base + p] = 1.0
            pk[p, h, base + N_SPLIT + p] = 1.0
            cq[0, base + N_SPLIT + p] = 1.0
        hsel[h, :, base:base + GATE_COLS] = 1.0
    return (jnp.asarray(pq, BF16), jnp.asarray(pk, BF16), jnp.asarray(cq, F32),
            jnp.asarray(ck, F32), jnp.asarray(hsel, BF16))


def _fox_gate_kernel(h_ref, wf_ref, bf_ref, pq_ref, pk_ref, cq_ref, ck_ref, qx_ref, kx_ref):
    x = jnp.dot(h_ref[...], wf_ref[...], preferred_element_type=F32) + bf_ref[...]
    lf = -(jnp.maximum(-x, 0.0) + jnp.log1p(jnp.exp(-jnp.abs(x))))
    n = lf.shape[0]
    row = lax.broadcasted_iota(jnp.int32, lf.shape, 0)
    c = lf
    sh = 1
    while sh < n:
        c = c + jnp.where(row >= sh, pltpu.roll(c, sh, 0), 0.0)
        sh *= 2
    parts = _split3(c * LOG2E)
    qx = cq_ref[...]
    kx = ck_ref[...]
    for p in range(N_SPLIT):
        qx = qx + jnp.dot(parts[p], pq_ref[p], preferred_element_type=F32)
        kx = kx - jnp.dot(parts[p], pk_ref[p], preferred_element_type=F32)
    qx_ref[...] = qx.astype(BF16)
    kx_ref[...] = kx.astype(BF16)


def _fox_gate(h, wf, bf_row, consts, *, seq):
    m, d = h.shape
    pq, pk, cq, ck, _ = consts
    rows = pl.BlockSpec((seq, LANES), lambda b: (b, 0))
    vec = pl.BlockSpec((1, LANES), lambda b: (0, 0))
    mats = pl.BlockSpec((N_SPLIT, LANES, LANES), lambda b: (0, 0, 0))
    vmem = 2 * seq * d * 2 + 16 * seq * LANES * 4
    return pl.pallas_call(
        _fox_gate_kernel,
        grid=(m // seq,),
        in_specs=[pl.BlockSpec((seq, d), lambda b: (b, 0)),
                  pl.BlockSpec((d, LANES), lambda b: (0, 0)), vec, mats, mats, vec, vec],
        out_specs=[rows, rows],
        out_shape=[jax.ShapeDtypeStruct((m, LANES), BF16)] * 2,
        compiler_params=_params(1, vmem),
        name="fox_gate",
    )(h, wf, bf_row, pq, pk, cq, ck)


def _fox_attn_kernel(q_ref, k_ref, v_ref, qx_ref, kx_ref, hsel_ref, o_ref, kaug_ref, vaug_ref, *, tq):
    seq = q_ref.shape[0]
    kaug_ref[:, :HEAD_DIM] = k_ref[...]
    kaug_ref[:, HEAD_DIM:] = kx_ref[...] * hsel_ref[0:1, :]
    vaug_ref[:, :HEAD_DIM] = v_ref[...]
    vaug_ref[:, HEAD_DIM:] = jnp.ones((seq, HEAD_DIM), BF16)
    contract_last = (((1,), (1,)), ((), ()))
    r = lax.broadcasted_iota(jnp.int32, (tq, tq), 0)
    c = lax.broadcasted_iota(jnp.int32, (tq, tq), 1)
    causal = r >= c
    def scores(lo):
        rows = slice(lo, lo + tq)
        q_aug = jnp.concatenate([q_ref[rows, :], qx_ref[rows, :]], axis=1)
        s = lax.dot_general(q_aug, kaug_ref[0:lo + tq, :], contract_last, preferred_element_type=F32)
        diag = jnp.where(causal, s[:, lo:], -jnp.inf)
        return jnp.concatenate([s[:, :lo], diag], axis=1) if lo else diag

    nxt = scores(0)
    for lo in range(0, seq, tq):
        s = nxt
        if lo + tq < seq:
            nxt = scores(lo + tq)
        m = jnp.max(s, axis=-1, keepdims=True)
        p = jnp.exp2(s - m)
        acc = jnp.dot(p.astype(BF16), vaug_ref[0:lo + tq, :], preferred_element_type=F32)
        o_ref[lo:lo + tq, :] = (acc[:, :HEAD_DIM] / acc[:, HEAD_DIM:]).astype(o_ref.dtype)


def _fox_attn(qkv, qx, kx, hsel, *, batch, seq, tq=256):
    d_model = N_HEADS * HEAD_DIM
    full = lambda off: pl.BlockSpec((seq, HEAD_DIM), lambda b, h: (b, h + off))
    gate = pl.BlockSpec((seq, LANES), lambda b, h: (b, 0))
    vmem = 2 * 6 * seq * HEAD_DIM * 2 + seq * 2 * HEAD_DIM * 2 + 6 * tq * seq * 4
    return pl.pallas_call(
        functools.partial(_fox_attn_kernel, tq=tq),
        grid=(batch, N_HEADS),
        in_specs=[full(0), full(N_HEADS), full(2 * N_HEADS), gate, gate,
                  pl.BlockSpec((None, BF16_ROWS, LANES), lambda b, h: (h, 0, 0))],
        out_specs=full(0),
        out_shape=jax.ShapeDtypeStruct((batch * seq, d_model), BF16),
        scratch_shapes=[pltpu.VMEM((seq, 2 * HEAD_DIM), BF16), pltpu.VMEM((seq, 2 * HEAD_DIM), BF16)],
        compiler_params=_params(2, vmem),
        name="fox_attn",
    )(qkv, qkv, qkv, qx, kx, hsel)


def _rel_bias_kernel(rb_ref, o_ref):
    rb = jnp.broadcast_to(rb_ref[...], (8, rb_ref.shape[-1]))
    nrb = rb.shape[1]
    ridx = lax.broadcasted_iota(jnp.int32, (nrb, REL_ROLL_W), 0)
    mcol = lax.broadcasted_iota(jnp.int32, (nrb, REL_ROLL_W), 1)
    off = jnp.where(mcol < REL_TK, mcol, mcol - REL_ROLL_W)
    idx = jnp.clip(LEFT_CHUNKS * CHUNK - off, -REL_CLIP, REL_CLIP) + REL_CLIP
    onehot = jnp.where(ridx == idx, 1.0, 0.0).astype(BF16)
    vec = jnp.zeros((8, REL_ROLL_W), F32)
    for part in _split3(rb):
        vec = vec + jnp.dot(part, onehot, preferred_element_type=F32)
    tile = jnp.broadcast_to(vec[0:1, :], (REL_TQ, REL_ROLL_W))
    tile = pltpu.roll(tile, 0, 1, stride=1, stride_axis=0)[:, :REL_TK]
    qc = jnp.right_shift(lax.broadcasted_iota(jnp.int32, (REL_TQ, REL_TK), 0), 6)
    kc = jnp.right_shift(lax.broadcasted_iota(jnp.int32, (REL_TQ, REL_TK), 1), 6)
    valid = (kc >= qc) & (kc <= qc + LEFT_CHUNKS)
    o_ref[...] = jnp.where(valid, tile * LOG2E, -jnp.inf)


def _rel_bias_tiles(rel_bias):
    h, nb = rel_bias.shape
    nbp = -(-nb // LANES) * LANES
    rb = jnp.pad(rel_bias, ((0, 0), (0, nbp - nb))).reshape(h, 1, nbp)
    vmem = 8 * nbp * REL_ROLL_W * 4 + 8 * REL_TQ * REL_ROLL_W * 4
    return pl.pallas_call(
        _rel_bias_kernel,
        grid=(h,),
        in_specs=[pl.BlockSpec((None, 1, nbp), lambda i: (i, 0, 0))],
        out_specs=pl.BlockSpec((None, REL_TQ, REL_TK), lambda i: (i, 0, 0)),
        out_shape=jax.ShapeDtypeStruct((h, REL_TQ, REL_TK), F32),
        compiler_params=_params(1, vmem),
        name="rel_bias_tiles",
    )(rb)


def _rel_attn_kernel(q_ref, k_ref, v_ref, bias_ref, o_ref):
    seq = q_ref.shape[0]
    nqb = seq // REL_TQ

    def window(qb):
        return slice(max(qb - (REL_KBLOCKS - 1), 0) * REL_TQ, (qb + 1) * REL_TQ)

    def scores(qb):
        keys = window(qb)
        width = keys.stop - keys.start
        s = lax.dot_general(q_ref[qb * REL_TQ:(qb + 1) * REL_TQ, :], k_ref[keys, :],
                            (((1,), (1,)), ((), ())), preferred_element_type=F32)
        return s + bias_ref[:, REL_TK - width:]

    nxt = scores(0)
    for qb in range(nqb):
        s = nxt
        if qb + 1 < nqb:
            nxt = scores(qb + 1)
        m = jnp.max(s, axis=-1, keepdims=True)
        p = jnp.exp2(s - m)
        l = jnp.sum(p, axis=-1, keepdims=True)
        o = jnp.dot(p.astype(BF16), v_ref[window(qb), :], preferred_element_type=F32)
        o_ref[qb * REL_TQ:(qb + 1) * REL_TQ, :] = (o / l).astype(o_ref.dtype)


def _rel_attn(qkv, bias_tiles, *, batch, seq):
    d_model = N_HEADS * HEAD_DIM
    assert seq % REL_TQ == 0
    full = lambda off: pl.BlockSpec((seq, HEAD_DIM), lambda h, b: (b, h + off))
    vmem = 2 * (4 * seq * HEAD_DIM * 2 + REL_TQ * REL_TK * 4) + 8 * REL_TQ * REL_TK * 4
    return pl.pallas_call(
        _rel_attn_kernel,
        grid=(N_HEADS, batch),
        in_specs=[full(0), full(N_HEADS), full(2 * N_HEADS),
                  pl.BlockSpec((None, REL_TQ, REL_TK), lambda h, b: (h, 0, 0))],
        out_specs=full(0),
        out_shape=jax.ShapeDtypeStruct((batch * seq, d_model), BF16),
        compiler_params=_params(2, vmem),
        name="rel_attn",
    )(qkv, qkv, qkv, bias_tiles)


def _conv_tail_kernel(u_ref, halo_ref, wdw_ref, bdw_ref, cg_ref, cb_ref, w2_ref, b2_ref,
                      x_ref, g_ref, b_ref, of_ref, ob_ref, ext_ref, y0_ref, y1_ref, mix_ref, sh_ref,
                      *, tm, nt, tiles_per_seq):
    i = pl.program_id(0)
    d_model = u_ref.shape[1]
    base = CONV_HALO - (CONV_K - 1)
    taps = {r: [(q, 8 * q + r - base) for q in range(CONV_HALO // 8 + 1)
                if 0 <= 8 * q + r - base < CONV_K] for r in range(8)}

    n_chunks = d_model // LANES
    n_slabs = d_model // MXU_COLS
    chunks_per_slab = n_chunks // n_slabs

    def conv_stage():
        keep = jnp.where(i % tiles_per_seq == 0, 0.0, 1.0).astype(F32)
        ext_ref[0:CONV_HALO, :] = halo_ref[...] * keep
        ext_ref[CONV_HALO:CONV_HALO + tm, :] = u_ref[...]

    def conv_chunk(y_ref, ci):
        cols = slice(ci * LANES, (ci + 1) * LANES)
        buf = ci % 2
        for r in range(1, 8):
            span = 8 * max(q for q, _ in taps[r]) + tm
            sh_ref[buf, r - 1, 0:span, :] = ext_ref[r:r + span, cols]
        acc = jnp.broadcast_to(bdw_ref[:, cols], (tm, LANES))
        for r in range(8):
            for q, k in taps[r]:
                rows = slice(8 * q, 8 * q + tm)
                src = ext_ref[rows, cols] if r == 0 else sh_ref[buf, r - 1, rows, :]
                acc = acc + wdw_ref[k:k + 1, cols] * src
        y_ref[:, cols] = acc

    def activated(y_ref):
        y = _layer_norm(y_ref[...], cg_ref[...], cb_ref[...])
        return (y * jax.nn.sigmoid(y)).astype(BF16)

    def pointwise_slab(act, s):
        cols = slice(s * MXU_COLS, (s + 1) * MXU_COLS)
        mix_ref[:, cols] = jnp.dot(act, w2_ref[:, cols], preferred_element_type=F32) + b2_ref[:, cols]

    def finish():
        out = _layer_norm(ALPHA * x_ref[...] + mix_ref[...], g_ref[...], b_ref[...])
        of_ref[...] = out
        ob_ref[...] = out.astype(BF16)

    def conv(y_ref):
        conv_stage()
        for ci in range(n_chunks):
            conv_chunk(y_ref, ci)

    def tail(y_ref):
        act = activated(y_ref)
        for s in range(n_slabs):
            pointwise_slab(act, s)
        finish()

    def both(y_prev_ref, y_cur_ref):
        act = activated(y_prev_ref)
        conv_stage()
        for s in range(n_slabs):
            pointwise_slab(act, s)
            for ci in range(s * chunks_per_slab, (s + 1) * chunks_per_slab):
                conv_chunk(y_cur_ref, ci)
        finish()

    mid = jnp.logical_and(i > 0, i < nt)

    @pl.when(i == 0)
    def _():
        conv(y0_ref)

    @pl.when(jnp.logical_and(mid, i % 2 == 1))
    def _():
        both(y0_ref, y1_ref)

    @pl.when(jnp.logical_and(mid, i % 2 == 0))
    def _():
        both(y1_ref, y0_ref)

    @pl.when(i == nt)
    def _():
        tail(y1_ref if nt % 2 == 0 else y0_ref)


def _conv_tail(u, w_dw, b_dw, cg, cb, w2, layer, b2, x_res, ln_g, ln_b, *, seq, tm=256):
    m, d = u.shape
    assert seq % tm == 0 and tm % CONV_HALO == 0
    hb = tm // CONV_HALO
    nt = m // tm
    row = pl.BlockSpec((1, d), lambda i: (0, 0))
    cur = lambda i: jnp.minimum(i, nt - 1)
    lagged = pl.BlockSpec((tm, d), lambda i: (jnp.maximum(i - 1, 0), 0))
    wdw = jnp.pad(w_dw, ((0, CONV_HALO - CONV_K), (0, 0)))
    vmem = 2 * (tm * d * 4 * 3 + tm * d * 2 + CONV_HALO * d * 8) + d * d * 2 + (3 * tm + CONV_HALO) * d * 4 \
        + 4 * tm * d * 4
    return pl.pallas_call(
        functools.partial(_conv_tail_kernel, tm=tm, nt=nt, tiles_per_seq=seq // tm),
        grid=(nt + 1,),
        in_specs=[pl.BlockSpec((tm, d), lambda i: (cur(i), 0)),
                  pl.BlockSpec((CONV_HALO, d), lambda i: (jnp.maximum(cur(i) * hb - 1, 0), 0)),
                  pl.BlockSpec((CONV_HALO, d), lambda i: (0, 0)),
                  row, row, row,
                  pl.BlockSpec((d, d), lambda i: (layer, 0), pipeline_mode=pl.Buffered(1)),
                  row, lagged, row, row],
        out_specs=[lagged, lagged],
        out_shape=[jax.ShapeDtypeStruct((m, d), F32), jax.ShapeDtypeStruct((m, d), BF16)],
        scratch_shapes=[pltpu.VMEM((CONV_HALO + tm, d), F32), pltpu.VMEM((tm, d), F32),
                        pltpu.VMEM((tm, d), F32), pltpu.VMEM((tm, d), F32),
                        pltpu.VMEM((2, 7, CONV_HALO + tm, LANES), F32)],
        compiler_params=_params(1, vmem),
        name="conv_tail",
    )(u, u, wdw, b_dw, cg, cb, w2, b2, x_res, ln_g, ln_b)


def kernel(x, fox_w_qkv, fox_w_f, fox_b_f, fox_w_o, rel_w_qkv, rel_bias, rel_w_o, conv_w_pw1, conv_b_pw1, conv_w_dw, conv_b_dw, conv_ln_g, conv_ln_b, conv_w_pw2, conv_b_pw2, ffn_w_gate, ffn_w_up, ffn_w_down, ln_mix_g, ln_mix_b, ln_ffn_g, ln_ffn_b):
    batch, seq, d = x.shape
    m = batch * seq
    assert d == N_HEADS * HEAD_DIM
    row = lambda v: v.reshape(1, -1).astype(F32)
    xf = x.reshape(m, d)
    xb = xf.astype(BF16)
    gate_consts = _gate_placement()
    d_ff = ffn_w_gate.shape[2]
    fox_wo, rel_wo, pw2 = (_to_bf16_rows(w) for w in (fox_w_o, rel_w_o, conv_w_pw2))

    for i in range(DEPTH):
        kind, j = i % N_MIXERS, i // N_MIXERS
        mix_g, mix_b = row(ln_mix_g[i]), row(ln_mix_b[i])
        if kind == 0:
            wf = jnp.pad(fox_w_f[j], ((0, 0), (0, LANES - N_HEADS))).astype(BF16)
            bf = jnp.pad(fox_b_f[j], (0, LANES - N_HEADS)).reshape(1, LANES).astype(F32)
            qkv = _qkv_proj(xb, fox_w_qkv, j)
            qx, kx = _fox_gate(xb, wf, bf, gate_consts, seq=seq)
            o = _fox_attn(qkv, qx, kx, gate_consts[4], batch=batch, seq=seq)
            xf, xb = _mm_res_ln(o, fox_wo, j, xf, mix_g, mix_b, tn=d)
        elif kind == 1:
            qkv = _qkv_proj(xb, rel_w_qkv, j)
            tiles = _rel_bias_tiles(rel_bias[j])
            o = _rel_attn(qkv, tiles, batch=batch, seq=seq)
            xf, xb = _mm_res_ln(o, rel_wo, j, xf, mix_g, mix_b, tn=d)
        else:
            u = _gated_mm(xb, conv_w_pw1, conv_w_pw1, j, n_out=d, a_off=0, b_off=d, glu=True,
                          bias=row(conv_b_pw1[j]), out_dtype=F32)
            xf, xb = _conv_tail(u, conv_w_dw[j], row(conv_b_dw[j]), row(conv_ln_g[j]), row(conv_ln_b[j]),
                                pw2, j, row(conv_b_pw2[j]), xf, mix_g, mix_b, seq=seq)
        hdn, w_down = _gated_mm(xb, ffn_w_gate, ffn_w_up, i, n_out=d_ff, a_off=0, b_off=0, glu=False,
                                rider=ffn_w_down)
        xf, xb = _mm_res_ln(hdn, w_down, 0, xf, row(ln_ffn_g[i]), row(ln_ffn_b[i]),
                            want_bf16=i + 1 < DEPTH, tm=256, tn=d)
    return xf.reshape(batch, seq, d)
```
